```python
import math
import jax, jax.numpy as jnp
from jax import lax
import numpy as np

D_MODEL = 1024
BATCH = 16
SEQ = 2048
DEPTH = 1

D_CONV = D_MODEL // 2
D_ATTN = D_MODEL - D_CONV
HEAD_DIM = 64
N_HEADS = D_ATTN // HEAD_DIM
D_MIX = D_CONV + D_ATTN
D_IN_PROJ = 2 * D_CONV + 3 * D_ATTN
CONV_WIDTH = 31
DILATED_PATTERNS = ((128, 1), (512, 4), (2048, 16))
ATTN_BLOCK = 128
PEER_HEADS = 8
PEER_NKEYS = 128
PEER_QDIM = 256
PEER_TOPK = 16
PEER_CHUNK = 128
N_EXPERTS = PEER_NKEYS * PEER_NKEYS
ALPHA = (2.0 * DEPTH) ** 0.25
BETA = (8.0 * DEPTH) ** -0.25
LN_EPS = 1e-5
NEG_INF = -1e30

kernel_name = "hymba_conformer_dilated_alibi_peer_deepnorm"


def layer_norm(x, g, b):
    xf = x.astype(jnp.float32)
    mu = jnp.mean(xf, axis=-1, keepdims=True)
    var = jnp.mean(jnp.square(xf - mu), axis=-1, keepdims=True)
    return ((xf - mu) * lax.rsqrt(var + LN_EPS)).astype(x.dtype) * g + b


def alibi_slopes():
    return 2.0 ** (-8.0 * jnp.arange(1, N_HEADS + 1, dtype=jnp.float32) / N_HEADS)


def conformer_conv(u, w_dw, b_dw, g, b):
    a, gate = jnp.split(u, 2, axis=-1)
    hdn = a * jax.nn.sigmoid(gate)
    hdn = lax.conv_general_dilated(
        hdn, w_dw[:, None, :], window_strides=(1,),
        padding=((CONV_WIDTH - 1, 0),),
        dimension_numbers=("NWC", "WIO", "NWC"),
        feature_group_count=D_CONV) + b_dw
    hdn = layer_norm(hdn, g, b)
    return jax.nn.silu(hdn)


def dilated_branch(q, k, v, slopes, window, dilation):
    B, S, H, dh = q.shape
    L = S // dilation
    n_back = window // dilation
    nb = -(-L // ATTN_BLOCK)
    pad = nb * ATTN_BLOCK - L
    n_prev = -(-n_back // ATTN_BLOCK)
    kw = (n_prev + 1) * ATTN_BLOCK

    def to_stream(t):
        t = t.reshape(B, L, dilation, H, dh).transpose(0, 2, 3, 1, 4)
        t = jnp.pad(t, ((0, 0), (0, 0), (0, 0), (0, pad), (0, 0)))
        return t.reshape(B, dilation, H, nb, ATTN_BLOCK, dh)

    def band(t):
        tp = jnp.pad(t, ((0, 0), (0, 0), (0, 0), (n_prev, 0), (0, 0), (0, 0)))
        return jnp.concatenate([tp[:, :, :, i:i + nb] for i in range(n_prev + 1)], axis=4)

    qb = to_stream(q * (HEAD_DIM ** -0.5))
    kb = band(to_stream(k))
    vb = band(to_stream(v))

    q_loc = n_prev * ATTN_BLOCK + jnp.arange(ATTN_BLOCK)[:, None]
    k_loc = jnp.arange(kw)[None, :]
    dist = q_loc - k_loc
    j_glob = (jnp.arange(nb)[:, None, None] - n_prev) * ATTN_BLOCK + k_loc[None]
    valid = ((dist >= 0) & (dist <= n_back))[None] & (j_glob >= 0)
    bias = -slopes[:, None, None, None] * (dist * dilation).astype(jnp.float32)[None, None]

    s = jnp.einsum("bdhnqc,bdhnkc->bdhnqk", qb, kb, preferred_element_type=jnp.float32) + bias
    s = jnp.where(valid, s, NEG_INF)
    m = jnp.max(s, axis=-1, keepdims=True)
    p = jnp.exp(s - m)
    l = jnp.sum(p, axis=-1, keepdims=True)
    o = jnp.einsum("bdhnqk,bdhnkc->bdhnqc", p, vb.astype(jnp.float32)) / l
    lse = (m + jnp.log(l))[..., 0]

    o = o.reshape(B, dilation, H, nb * ATTN_BLOCK, dh)[:, :, :, :L]
    o = o.transpose(0, 3, 1, 2, 4).reshape(B, S, H, dh)
    lse = lse.reshape(B, dilation, H, nb * ATTN_BLOCK)[:, :, :, :L]
    lse = lse.transpose(0, 3, 1, 2).reshape(B, S, H)
    return o, lse


def dilated_attention(q, k, v):
    slopes = alibi_slopes()
    outs, lses = [], []
    for window, dilation in DILATED_PATTERNS:
        o, lse = dilated_branch(q, k, v, slopes, window, dilation)
        outs.append(o)
        lses.append(lse)
    w = jax.nn.softmax(jnp.stack(lses, axis=0), axis=0)
    out = jnp.sum(w[..., None] * jnp.stack(outs, axis=0), axis=0)
    return out.astype(q.dtype)


def peer(h, w_q, sub_keys, u_tab, v_tab):
    B, S, D = h.shape
    tokens = h.reshape(-1, PEER_CHUNK, D)

    def chunk_fn(xc):
        q = (xc @ w_q).reshape(PEER_CHUNK, PEER_HEADS, 2, PEER_QDIM // 2)
        s = jnp.einsum("thpc,hpnc->thpn", q, sub_keys, preferred_element_type=jnp.float32)
        top_s, top_i = lax.top_k(s, PEER_TOPK)
        cand_s = top_s[:, :, 0, :, None] + top_s[:, :, 1, None, :]
        cand_i = top_i[:, :, 0, :, None] * PEER_NKEYS + top_i[:, :, 1, None, :]
        cand_s = cand_s.reshape(PEER_CHUNK, PEER_HEADS, PEER_TOPK * PEER_TOPK)
        cand_i = cand_i.reshape(PEER_CHUNK, PEER_HEADS, PEER_TOPK * PEER_TOPK)
        best_s, best_pos = lax.top_k(cand_s, PEER_TOPK)
        idx = jnp.take_along_axis(cand_i, best_pos, axis=-1)
        g = jax.nn.softmax(best_s, axis=-1)
        u = u_tab[idx]
        act = jax.nn.gelu(jnp.einsum("thkd,td->thk", u, xc), approximate=False)
        v = v_tab[idx]
        return jnp.einsum("thk,thkd->td", (g * act).astype(xc.dtype), v)

    return lax.map(chunk_fn, tokens).reshape(B, S, D)


def setup_inputs(seed: int = 0) -> dict:
    key = jax.random.key(seed)
    ks = jax.random.split(key, 20)
    f32 = jnp.float32
    nrm = lambda k, shape: jax.random.normal(k, shape, dtype=f32)
    return {
        "x": nrm(ks[0], (BATCH, SEQ, D_MODEL)),
        "w_in": nrm(ks[1], (DEPTH, D_MODEL, D_IN_PROJ)) * D_MODEL ** -0.5,
        "b_in": nrm(ks[2], (DEPTH, D_IN_PROJ)) * 0.01,
        "conv_w": nrm(ks[3], (DEPTH, CONV_WIDTH, D_CONV)) * CONV_WIDTH ** -0.5,
        "conv_b": nrm(ks[4], (DEPTH, D_CONV)) * 0.01,
        "conv_ln_g": 1.0 + 0.01 * nrm(ks[5], (DEPTH, D_CONV)),
        "conv_ln_b": nrm(ks[6], (DEPTH, D_CONV)) * 0.01,
        "w_out": nrm(ks[7], (DEPTH, D_MIX, D_MODEL)) * (D_MIX ** -0.5) * BETA,
        "b_out": nrm(ks[8], (DEPTH, D_MODEL)) * 0.01,
        "ln1_g": 1.0 + 0.01 * nrm(ks[9], (DEPTH, D_MODEL)),
        "ln1_b": nrm(ks[10], (DEPTH, D_MODEL)) * 0.01,
        "peer_wq": nrm(ks[11], (DEPTH, D_MODEL, PEER_HEADS * PEER_QDIM)) * D_MODEL ** -0.5,
        "peer_keys": nrm(ks[12], (DEPTH, PEER_HEADS, 2, PEER_NKEYS, PEER_QDIM // 2)) * (PEER_QDIM // 2) ** -0.5,
        "peer_u": nrm(ks[13], (DEPTH, N_EXPERTS, D_MODEL)) * D_MODEL ** -0.5,
        "peer_v": nrm(ks[14], (DEPTH, N_EXPERTS, D_MODEL)) * BETA * PEER_HEADS ** -0.5,
        "ln2_g": 1.0 + 0.01 * nrm(ks[15], (DEPTH, D_MODEL)),
        "ln2_b": nrm(ks[16], (DEPTH, D_MODEL)) * 0.01,
    }


def reference(x, w_in, b_in, conv_w, conv_b, conv_ln_g, conv_ln_b, w_out, b_out,
              ln1_g, ln1_b, peer_wq, peer_keys, peer_u, peer_v, ln2_g, ln2_b):
    B, S, _ = x.shape
    h = x
    for l in range(DEPTH):
        proj = h @ w_in[l] + b_in[l]
        u_conv, q, k, v = jnp.split(proj, [2 * D_CONV, 2 * D_CONV + D_ATTN, 2 * D_CONV + 2 * D_ATTN], axis=-1)
        y_conv = conformer_conv(u_conv, conv_w[l], conv_b[l], conv_ln_g[l], conv_ln_b[l])
        q = q.reshape(B, S, N_HEADS, HEAD_DIM)
        k = k.reshape(B, S, N_HEADS, HEAD_DIM)
        v = v.reshape(B, S, N_HEADS, HEAD_DIM)
        y_attn = dilated_attention(q, k, v).reshape(B, S, D_ATTN)
        y = jnp.concatenate([y_conv, y_attn], axis=-1) @ w_out[l] + b_out[l]
        h = layer_norm(ALPHA * h + y, ln1_g[l], ln1_b[l])
        y = peer(h, peer_wq[l], peer_keys[l], peer_u[l], peer_v[l])
        h = layer_norm(ALPHA * h + y, ln2_g[l], ln2_b[l])
    return h
```

```python
import functools
import math

import jax
import jax.numpy as jnp
from jax import lax
from jax.experimental import pallas as pl
from jax.experimental.pallas import tpu as pltpu

F32 = jnp.float32
BF16 = jnp.bfloat16
I32 = jnp.int32

D_MODEL = 1024
D_CONV = D_MODEL // 2
D_ATTN = D_MODEL - D_CONV
HEAD_DIM = 64
N_HEADS = D_ATTN // HEAD_DIM
CONV_WIDTH = 31
DILATED_PATTERNS = ((128, 1), (512, 4), (2048, 16))
ATTN_BLOCK = 128
PEER_HEADS = 8
PEER_NKEYS = 128
PEER_QDIM = 256
PEER_TOPK = 16
LN_EPS = 1e-5
NEG_INF = -1e30

LANES = 128
SUBLANES = 8
VMEM_LIMIT = 56 * 1024 * 1024

PEER_SLOTS = 4
PEER_TOKENS = 128


def _ln(x, g, b):
    mu = jnp.mean(x, axis=-1, keepdims=True)
    xc = x - mu
    var = jnp.mean(xc * xc, axis=-1, keepdims=True)
    return xc * lax.rsqrt(var + LN_EPS) * g + b


def _in_proj_kernel(x_ref, w_ref, b_ref, u_ref, qkv_ref):
    x = x_ref[...].astype(BF16)
    nu = u_ref.shape[1]
    u_ref[...] = jnp.dot(x, w_ref[:, :nu], preferred_element_type=F32) + b_ref[:, :nu]
    qkv_ref[...] = jnp.dot(x, w_ref[:, nu:], preferred_element_type=F32) + b_ref[:, nu:]


def _in_proj(x2, w_bf, b, tm=256):
    n, d = x2.shape
    dout = w_bf.shape[1]
    nu = 2 * D_CONV
    return pl.pallas_call(
        _in_proj_kernel,
        grid=(n // tm,),
        in_specs=[
            pl.BlockSpec((tm, d), lambda i: (i, 0)),
            pl.BlockSpec((d, dout), lambda i: (0, 0)),
            pl.BlockSpec((1, dout), lambda i: (0, 0)),
        ],
        out_specs=[
            pl.BlockSpec((tm, nu), lambda i: (i, 0)),
            pl.BlockSpec((tm, dout - nu), lambda i: (i, 0)),
        ],
        out_shape=[
            jax.ShapeDtypeStruct((n, nu), F32),
            jax.ShapeDtypeStruct((n, dout - nu), F32),
        ],
        compiler_params=pltpu.CompilerParams(
            dimension_semantics=("arbitrary",), vmem_limit_bytes=VMEM_LIMIT),
        name="in_proj",
    )(x2, w_bf, b.reshape(1, dout))


CONV_ROWS = 64
CONV_PAD = 32


def _conv_kernel(u_ref, w_ref, cb_ref, g_ref, b_ref, o_ref, pad_ref):
    s = o_ref.shape[0]
    pad_ref[0:CONV_PAD, :] = jnp.zeros((CONV_PAD, D_CONV), F32)

    def glu(i, c):
        r0 = pl.multiple_of(i * 256, 256)
        a = u_ref[pl.ds(r0, 256), 0:D_CONV]
        gate = u_ref[pl.ds(r0, 256), D_CONV:2 * D_CONV]
        pad_ref[pl.ds(CONV_PAD + r0, 256), :] = a * jax.nn.sigmoid(gate)
        return c

    lax.fori_loop(0, s // 256, glu, 0)

    off = CONV_PAD - (CONV_WIDTH - 1)

    def tile(i, c):
        r0 = pl.multiple_of(i * CONV_ROWS, CONV_ROWS)
        acc = jnp.broadcast_to(cb_ref[...], (CONV_ROWS, D_CONV))
        win = pad_ref[pl.ds(r0, CONV_ROWS + CONV_PAD), :]
        for j in range(CONV_WIDTH):
            acc = acc + w_ref[j:j + 1, :] * win[off + j:off + j + CONV_ROWS, :]
        y = _ln(acc, g_ref[...], b_ref[...])
        o_ref[pl.ds(r0, CONV_ROWS), :] = (y * jax.nn.sigmoid(y)).astype(o_ref.dtype)
        return c

    lax.fori_loop(0, s // CONV_ROWS, tile, 0)


def _conv(u, conv_w, conv_b, g, b, batch, seq):
    n = u.shape[0]
    return pl.pallas_call(
        _conv_kernel,
        grid=(batch,),
        in_specs=[
            pl.BlockSpec((seq, 2 * D_CONV), lambda i: (i, 0)),
            pl.BlockSpec((CONV_WIDTH, D_CONV), lambda i: (0, 0)),
            pl.BlockSpec((1, D_CONV), lambda i: (0, 0)),
            pl.BlockSpec((1, D_CONV), lambda i: (0, 0)),
            pl.BlockSpec((1, D_CONV), lambda i: (0, 0)),
        ],
        out_specs=pl.BlockSpec((seq, D_CONV), lambda i: (i, 0)),
        out_shape=jax.ShapeDtypeStruct((n, D_CONV), BF16),
        scratch_shapes=[pltpu.VMEM((seq + CONV_PAD, D_CONV), F32)],
        compiler_params=pltpu.CompilerParams(
            dimension_semantics=("arbitrary",), vmem_limit_bytes=VMEM_LIMIT),
        name="conv",
    )(u, conv_w, conv_b.reshape(1, -1), g.reshape(1, -1), b.reshape(1, -1))


def _attn_kernel(slopes_ref, q_ref, k_ref, v_ref, o_ref, bias_ref, oacc_ref, m_ref, l_ref):
    blk = ATTN_BLOCK
    s_len = q_ref.shape[0]
    hp = pl.program_id(1)
    n_pat = len(DILATED_PATTERNS)

    lane = lax.broadcasted_iota(I32, (blk, LANES), 1)
    head0 = lane < HEAD_DIM
    lane2 = lax.broadcasted_iota(I32, (2 * blk, LANES), 1)
    head0_2 = lane2 < HEAD_DIM

    qi = lax.broadcasted_iota(I32, (blk, 2 * blk), 0)
    kj = lax.broadcasted_iota(I32, (blk, 2 * blk), 1)
    dist = qi + blk - kj
    is_prev = (kj < blk).astype(F32)
    for p, (window, dil) in enumerate(DILATED_PATTERNS):
        n_back = window // dil
        valid = (dist >= 0) & (dist <= n_back)
        dpos = (dist * dil).astype(F32)
        for hh in range(2):
            slope = slopes_ref[2 * hp + hh]
            bias_ref[p, hh] = jnp.where(valid, -slope * dpos, NEG_INF)

    def unit(p, dil, cur, prev, first):
        def rows(ref, start):
            if dil == 1:
                return ref[pl.ds(start, blk), :]
            return ref[pl.ds(start, blk, stride=dil), :]

        qb = rows(q_ref, cur) * (HEAD_DIM ** -0.5)
        if prev is None:
            kc = rows(k_ref, cur)
            vc = rows(v_ref, cur)
            hmask = head0
        else:
            kc = jnp.concatenate([rows(k_ref, prev), rows(k_ref, cur)], axis=0)
            vc = jnp.concatenate([rows(v_ref, prev), rows(v_ref, cur)], axis=0)
            hmask = head0_2
            pen = jnp.where(first, NEG_INF, 0.0) * is_prev
        kb = kc.astype(BF16)
        o_full = None
        m_full = None
        l_full = None
        for hh in range(2):
            sel = head0 if hh == 0 else jnp.logical_not(head0)
            vsel = hmask if hh == 0 else jnp.logical_not(hmask)
            qh = jnp.where(sel, qb, 0.0).astype(BF16)
            vh = jnp.where(vsel, vc, 0.0).astype(BF16)
            sc = lax.dot_general(qh, kb, (((1,), (1,)), ((), ())), preferred_element_type=F32)
            if prev is None:
                sc = sc + bias_ref[p, hh, :, blk:2 * blk]
            else:
                sc = sc + bias_ref[p, hh] + pen
            m = jnp.max(sc, axis=1, keepdims=True)
            e = jnp.exp(sc - m)
            l = jnp.sum(e, axis=1, keepdims=True)
            o = jnp.dot(e.astype(BF16), vh, preferred_element_type=F32)
            mb = jnp.broadcast_to(m, (blk, LANES))
            lb = jnp.broadcast_to(l, (blk, LANES))
            if hh == 0:
                o_full, m_full, l_full = o, mb, lb
            else:
                o_full = o_full + o
                m_full = jnp.where(head0, m_full, mb)
                l_full = jnp.where(head0, l_full, lb)
        if dil == 1:
            idx = pl.ds(cur, blk)
        else:
            idx = pl.ds(cur, blk, stride=dil)
        oacc_ref[p, idx, :] = o_full
        m_ref[p, idx, :] = m_full
        l_ref[p, idx, :] = l_full

    for p, (window, dil) in enumerate(DILATED_PATTERNS):
        stream_len = s_len // dil
        nb = stream_len // blk

        if nb == 1:
            def body(r, c, p=p, dil=dil):
                unit(p, dil, r, None, None)
                return c
            lax.fori_loop(0, dil, body, 0)
        else:
            def body(u, c, p=p, dil=dil, nb=nb):
                r = u // nb
                i = u - r * nb
                cur = r + i * (blk * dil)
                prev = r + jnp.maximum(i - 1, 0) * (blk * dil)
                if dil == 1:
                    cur = pl.multiple_of(cur, blk)
                    prev = pl.multiple_of(prev, blk)
                unit(p, dil, cur, prev, i == 0)
                return c
            lax.fori_loop(0, dil * nb, body, 0)

    def merge(i, c):
        r0 = pl.multiple_of(i * 256, 256)
        ms = [m_ref[p, pl.ds(r0, 256), :] for p in range(n_pat)]
        mx = functools.reduce(jnp.maximum, ms)
        num = None
        den = None
        for p in range(n_pat):
            w = jnp.exp(ms[p] - mx)
            tn = w * oacc_ref[p, pl.ds(r0, 256), :]
            td = w * l_ref[p, pl.ds(r0, 256), :]
            num = tn if num is None else num + tn
            den = td if den is None else den + td
        o_ref[pl.ds(r0, 256), :] = (num / den).astype(o_ref.dtype)
        return c

    lax.fori_loop(0, s_len // 256, merge, 0)


def _attention(qkv, slopes, batch, seq):
    n = qkv.shape[0]
    n_hp = D_ATTN // LANES
    n_pat = len(DILATED_PATTERNS)
    return pl.pallas_call(
        _attn_kernel,
        grid=(batch, n_hp),
        in_specs=[
            pl.BlockSpec(memory_space=pltpu.SMEM),
            pl.BlockSpec((seq, LANES), lambda b, h: (b, h)),
            pl.BlockSpec((seq, LANES), lambda b, h: (b, n_hp + h)),
            pl.BlockSpec((seq, LANES), lambda b, h: (b, 2 * n_hp + h)),
        ],
        out_specs=pl.BlockSpec((seq, LANES), lambda b, h: (b, h)),
        out_shape=jax.ShapeDtypeStruct((n, D_ATTN), BF16),
        scratch_shapes=[
            pltpu.VMEM((n_pat, 2, ATTN_BLOCK, 2 * ATTN_BLOCK), F32),
            pltpu.VMEM((n_pat, seq, LANES), F32),
            pltpu.VMEM((n_pat, seq, LANES), F32),
            pltpu.VMEM((n_pat, seq, LANES), F32),
        ],
        compiler_params=pltpu.CompilerParams(
            dimension_semantics=("arbitrary", "arbitrary"), vmem_limit_bytes=VMEM_LIMIT),
        name="attn",
    )(slopes, qkv, qkv, qkv)


def _out_proj_kernel(alpha, yc_ref, ya_ref, x_ref, wc_ref, wa_ref, b_ref, g_ref, be_ref, o_ref):
    y = jnp.dot(yc_ref[...], wc_ref[...], preferred_element_type=F32)
    y = y + jnp.dot(ya_ref[...], wa_ref[...], preferred_element_type=F32)
    y = y + b_ref[...]
    o_ref[...] = _ln(alpha * x_ref[...] + y, g_ref[...], be_ref[...])


def _out_proj(yc, ya, x2, w_out_bf, b_out, g, b, alpha, tm=256):
    n, d = x2.shape
    row = lambda i: (i, 0)
    fix = lambda i: (0, 0)
    return pl.pallas_call(
        functools.partial(_out_proj_kernel, alpha),
        grid=(n // tm,),
        in_specs=[
            pl.BlockSpec((tm, D_CONV), row),
            pl.BlockSpec((tm, D_ATTN), row),
            pl.BlockSpec((tm, d), row),
            pl.BlockSpec((D_CONV, d), fix),
            pl.BlockSpec((D_ATTN, d), fix),
            pl.BlockSpec((1, d), fix),
            pl.BlockSpec((1, d), fix),
            pl.BlockSpec((1, d), fix),
        ],
        out_specs=pl.BlockSpec((tm, d), row),
        out_shape=jax.ShapeDtypeStruct((n, d), F32),
        compiler_params=pltpu.CompilerParams(
            dimension_semantics=("arbitrary",), vmem_limit_bytes=VMEM_LIMIT),
        name="out_proj",
    )(yc, ya, x2, w_out_bf[:D_CONV], w_out_bf[D_CONV:], b_out.reshape(1, d),
      g.reshape(1, d), b.reshape(1, d))


ROUTE_TOKENS = 256


def _route_kernel(h_ref, wq_ref, keys_ref, idx_ref, g_ref, q_scr, gt_scr, it_scr):
    t = h_ref.shape[0]
    k_top = PEER_TOPK
    nk = PEER_NKEYS
    q_scr[...] = jnp.dot(h_ref[...].astype(BF16), wq_ref[...], preferred_element_type=F32)

    iota_n = lax.broadcasted_iota(I32, (nk, t), 0)
    iota_k = lax.broadcasted_iota(I32, (k_top, t), 0)
    pos = lax.broadcasted_iota(I32, (k_top * k_top, t), 0)
    neg = -jnp.inf

    def head(h, c):
        top_s = []
        top_i = []
        for p in range(2):
            col = pl.multiple_of(h * PEER_QDIM + p * (PEER_QDIM // 2), PEER_QDIM // 2)
            q = q_scr[:, pl.ds(col, PEER_QDIM // 2)].astype(BF16)
            s = lax.dot_general(keys_ref[h, p], q, (((1,), (1,)), ((), ())),
                                preferred_element_type=F32)
            ts = jnp.zeros((k_top, t), F32)
            ti = jnp.zeros((k_top, t), I32)
            for k in range(k_top):
                m = jnp.max(s, axis=0, keepdims=True)
                am = jnp.min(jnp.where(s == m, iota_n, nk), axis=0, keepdims=True)
                s = jnp.where(iota_n == am, neg, s)
                ts = jnp.where(iota_k == k, m, ts)
                ti = jnp.where(iota_k == k, am, ti)
            top_s.append(ts)
            top_i.append(ti)
        cand = jnp.concatenate(
            [top_s[0][i:i + 1, :] + top_s[1] for i in range(k_top)], axis=0)
        eid = jnp.concatenate(
            [top_i[0][i:i + 1, :] * nk + top_i[1] for i in range(k_top)], axis=0)
        bs = jnp.zeros((k_top, t), F32)
        be = jnp.zeros((k_top, t), I32)
        for k in range(k_top):
            m = jnp.max(cand, axis=0, keepdims=True)
            am = jnp.min(jnp.where(cand == m, pos, k_top * k_top), axis=0, keepdims=True)
            sel = pos == am
            e = jnp.max(jnp.where(sel, eid, -1), axis=0, keepdims=True)
            cand = jnp.where(sel, neg, cand)
            bs = jnp.where(iota_k == k, m, bs)
            be = jnp.where(iota_k == k, e, be)
        ex = jnp.exp(bs - bs[0:1, :])
        gate = ex / jnp.sum(ex, axis=0, keepdims=True)
        r0 = pl.multiple_of(h * k_top, k_top)
        gt_scr[pl.ds(r0, k_top), :] = gate
        it_scr[pl.ds(r0, k_top), :] = be
        return c

    lax.fori_loop(0, PEER_HEADS, head, 0)
    g_ref[...] = gt_scr[...].T
    idx_ref[...] = it_scr[...].T


def _route(h1, wq_bf, keys_bf, tm=ROUTE_TOKENS):
    n, d = h1.shape
    hk = PEER_HEADS * PEER_TOPK
    return pl.pallas_call(
        _route_kernel,
        grid=(n // tm,),
        in_specs=[
            pl.BlockSpec((tm, d), lambda i: (i, 0)),
            pl.BlockSpec(wq_bf.shape, lambda i: (0, 0)),
            pl.BlockSpec(keys_bf.shape, lambda i: (0, 0, 0, 0)),
        ],
        out_specs=[
            pl.BlockSpec((tm, hk), lambda i: (i, 0)),
            pl.BlockSpec((tm, hk), lambda i: (i, 0)),
        ],
        out_shape=[
            jax.ShapeDtypeStruct((n, hk), I32),
            jax.ShapeDtypeStruct((n, hk), F32),
        ],
        scratch_shapes=[
            pltpu.VMEM((tm, PEER_HEADS * PEER_QDIM), F32),
            pltpu.VMEM((hk, tm), F32),
            pltpu.VMEM((hk, tm), I32),
        ],
        compiler_params=pltpu.CompilerParams(
            dimension_semantics=("arbitrary",), vmem_limit_bytes=VMEM_LIMIT),
        name="route",
    )(h1, wq_bf, keys_bf)


def _peer_kernel(alpha, idx_ref, h_ref, g_ref, uv_ref, lg_ref, lb_ref, o_ref, *scratch):
    ns = PEER_SLOTS
    bufs = scratch[:ns]
    sem = scratch[ns]
    y_scr = scratch[ns + 1]
    tb = h_ref.shape[0]
    hk = g_ref.shape[1]
    d = h_ref.shape[1]
    n_lt = d // LANES

    def row_copy(tok, k, slot):
        e = idx_ref[tok * hk + k]
        return pltpu.make_async_copy(
            uv_ref.at[pl.ds(e, 1), :], bufs[slot].at[pl.ds(k, 1), :], sem.at[slot])

    def issue(tok, slot):
        for k in range(hk):
            row_copy(tok, k, slot).start(priority=k % 2)

    def wait(slot):
        pltpu.make_async_copy(uv_ref.at[pl.ds(0, hk), :], bufs[slot], sem.at[slot]).wait()

    eye = (lax.broadcasted_iota(I32, (hk, hk), 0) == lax.broadcasted_iota(I32, (hk, hk), 1))
    ones_row = jnp.ones((SUBLANES, hk), BF16)
    ones_sq = jnp.ones((3 * hk, LANES), BF16)

    def compute(tok, slot):
        buf = bufs[slot]
        x_row = h_ref[pl.ds(tok, 1), :]
        acc = None
        for c in range(n_lt):
            xc = jnp.broadcast_to(x_row[:, c * LANES:(c + 1) * LANES], (hk, LANES))
            term = buf[:, c * LANES:(c + 1) * LANES] * xc
            acc = term if acc is None else acc + term
        hi = acc.astype(BF16)
        lo = (acc - hi.astype(F32)).astype(BF16)
        nt = (((1,), (1,)), ((), ()))
        a = (lax.dot_general(ones_row, hi, nt, preferred_element_type=F32)
             + lax.dot_general(ones_row, lo, nt, preferred_element_type=F32))
        act = 0.5 * a * (1.0 + lax.erf(a * (2.0 ** -0.5)))
        w = act * g_ref[pl.ds(tok, 1), :]
        w_hi = w.astype(BF16).astype(F32)
        r1 = w - w_hi
        w_mid = r1.astype(BF16).astype(F32)
        w_lo = r1 - w_mid
        diag = jnp.concatenate(
            [jnp.where(eye, jnp.broadcast_to(part[0:1, :], (hk, hk)), 0.0).astype(BF16)
             for part in (w_hi, w_mid, w_lo)], axis=1)
        wb = jnp.dot(diag, ones_sq, preferred_element_type=F32)
        return [jnp.sum(buf[:, d + c * LANES:d + (c + 1) * LANES] * wb, axis=0, keepdims=True)
                for c in range(n_lt)]

    for j in range(ns - 1):
        issue(j, j)

    grp = SUBLANES
    sub = lax.broadcasted_iota(I32, (grp, LANES), 0)

    def group(gi, c):
        base = pl.multiple_of(gi * grp, grp)
        y8 = [jnp.zeros((grp, LANES), F32) for _ in range(n_lt)]
        for j in range(grp):
            tok = base + j
            nxt = tok + ns - 1

            @pl.when(nxt < tb)
            def _():
                issue(nxt, (j + ns - 1) % ns)

            wait(j % ns)
            rows = compute(tok, j % ns)
            y8 = [jnp.where(sub == j, jnp.broadcast_to(rows[cc], (grp, LANES)), y8[cc])
                  for cc in range(n_lt)]
        for cc in range(n_lt):
            y_scr[pl.ds(base, grp), cc * LANES:(cc + 1) * LANES] = y8[cc]
        return c

    lax.fori_loop(0, tb // grp, group, 0)
    o_ref[...] = _ln(alpha * h_ref[...] + y_scr[...], lg_ref[...], lb_ref[...])


def _peer(idx_flat, h1, gates, uv, g, b, alpha, tb=PEER_TOKENS):
    n, d = h1.shape
    hk = gates.shape[1]
    row = lambda i: (i, 0)
    fix = lambda i: (0, 0)
    return pl.pallas_call(
        functools.partial(_peer_kernel, alpha),
        grid=(n // tb,),
        in_specs=[
            pl.BlockSpec((tb * hk,), lambda i: (i,), memory_space=pltpu.SMEM),
            pl.BlockSpec((tb, d), row),
            pl.BlockSpec((tb, hk), row),
            pl.BlockSpec(memory_space=pl.ANY),
            pl.BlockSpec((1, d), fix),
            pl.BlockSpec((1, d), fix),
        ],
        out_specs=pl.BlockSpec((tb, d), row),
        out_shape=jax.ShapeDtypeStruct((n, d), F32),
        scratch_shapes=(
            [pltpu.VMEM((hk, 2 * d), F32) for _ in range(PEER_SLOTS)]
            + [pltpu.SemaphoreType.DMA((PEER_SLOTS,)), pltpu.VMEM((tb, d), F32)]),
        compiler_params=pltpu.CompilerParams(
            dimension_semantics=("arbitrary",), vmem_limit_bytes=VMEM_LIMIT),
        name="peer",
    )(idx_flat, h1, gates, uv, g.reshape(1, d), b.reshape(1, d))


def kernel(x, w_in, b_in, conv_w, conv_b, conv_ln_g, conv_ln_b, w_out, b_out, ln1_g, ln1_b,
           peer_wq, peer_keys, peer_u, peer_v, ln2_g, ln2_b):
    batch, seq, d = x.shape
    depth = w_in.shape[0]
    alpha = (2.0 * depth) ** 0.25
    slopes = 2.0 ** (-8.0 * jnp.arange(1, N_HEADS + 1, dtype=F32) / N_HEADS)
    h = x.reshape(batch * seq, d)
    for l in range(depth):
        u, qkv = _in_proj(h, w_in[l].astype(BF16), b_in[l])
        yc = _conv(u, conv_w[l], conv_b[l], conv_ln_g[l], conv_ln_b[l], batch, seq)
        ya = _attention(qkv, slopes, batch, seq)
        h1 = _out_proj(yc, ya, h, w_out[l].astype(BF16), b_out[l], ln1_g[l], ln1_b[l], alpha)
        idx, gates = _route(h1, peer_wq[l].astype(BF16), peer_keys[l].astype(BF16))
        uv = jnp.concatenate([peer_u[l], peer_v[l]], axis=1)
        h = _peer(idx.reshape(-1), h1, gates, uv, ln2_g[l], ln2_b[l], alpha)
    return h.reshape(batch, seq, d)
```

```python
import functools
import math

import jax
import jax.numpy as jnp
from jax import lax
from jax.experimental import pallas as pl
from jax.experimental.pallas import tpu as pltpu

F32 = jnp.float32
BF16 = jnp.bfloat16
I32 = jnp.int32

D_MODEL = 1024
D_CONV = D_MODEL // 2
D_ATTN = D_MODEL - D_CONV
HEAD_DIM = 64
N_HEADS = D_ATTN // HEAD_DIM
CONV_WIDTH = 31
DILATED_PATTERNS = ((128, 1), (512, 4), (2048, 16))
ATTN_BLOCK = 128
PEER_HEADS = 8
PEER_NKEYS = 128
PEER_QDIM = 256
PEER_TOPK = 16
LN_EPS = 1e-5
NEG_INF = -1e30

LANES = 128
SUBLANES = 8
VMEM_LIMIT = 56 * 1024 * 1024

PEER_SET = 4
PEER_ALLOCS = 4
PEER_AHEAD = 2 * PEER_SET
PEER_TOKENS = 128


def _ln(x, g, b):
    mu = jnp.mean(x, axis=-1, keepdims=True)
    xc = x - mu
    var = jnp.mean(xc * xc, axis=-1, keepdims=True)
    return xc * lax.rsqrt(var + LN_EPS) * g + b


def _in_proj_kernel(x_ref, w_ref, b_ref, u_ref, qkv_ref):
    x = x_ref[...].astype(BF16)
    nu = u_ref.shape[1]
    u_ref[...] = jnp.dot(x, w_ref[:, :nu], preferred_element_type=F32) + b_ref[:, :nu]
    qkv_ref[...] = jnp.dot(x, w_ref[:, nu:], preferred_element_type=F32) + b_ref[:, nu:]


def _in_proj(x2, w_bf, b, tm=256):
    n, d = x2.shape
    dout = w_bf.shape[1]
    nu = 2 * D_CONV
    return pl.pallas_call(
        _in_proj_kernel,
        grid=(n // tm,),
        in_specs=[
            pl.BlockSpec((tm, d), lambda i: (i, 0)),
            pl.BlockSpec((d, dout), lambda i: (0, 0)),
            pl.BlockSpec((1, dout), lambda i: (0, 0)),
        ],
        out_specs=[
            pl.BlockSpec((tm, nu), lambda i: (i, 0)),
            pl.BlockSpec((tm, dout - nu), lambda i: (i, 0)),
        ],
        out_shape=[
            jax.ShapeDtypeStruct((n, nu), F32),
            jax.ShapeDtypeStruct((n, dout - nu), F32),
        ],
        compiler_params=pltpu.CompilerParams(
            dimension_semantics=("arbitrary",), vmem_limit_bytes=VMEM_LIMIT),
        name="in_proj",
    )(x2, w_bf, b.reshape(1, dout))


CONV_ROWS = 64
CONV_PAD = 32


def _conv_kernel(u_ref, w_ref, cb_ref, g_ref, b_ref, o_ref, pad_ref):
    s = o_ref.shape[0]
    pad_ref[0:CONV_PAD, :] = jnp.zeros((CONV_PAD, D_CONV), F32)

    def glu(i, c):
        r0 = pl.multiple_of(i * 256, 256)
        a = u_ref[pl.ds(r0, 256), 0:D_CONV]
        gate = u_ref[pl.ds(r0, 256), D_CONV:2 * D_CONV]
        pad_ref[pl.ds(CONV_PAD + r0, 256), :] = a * jax.nn.sigmoid(gate)
        return c

    lax.fori_loop(0, s // 256, glu, 0)

    off = CONV_PAD - (CONV_WIDTH - 1)

    def tile(i, c):
        r0 = pl.multiple_of(i * CONV_ROWS, CONV_ROWS)
        acc = jnp.broadcast_to(cb_ref[...], (CONV_ROWS, D_CONV))
        win = pad_ref[pl.ds(r0, CONV_ROWS + CONV_PAD), :]
        for j in range(CONV_WIDTH):
            acc = acc + w_ref[j:j + 1, :] * win[off + j:off + j + CONV_ROWS, :]
        y = _ln(acc, g_ref[...], b_ref[...])
        o_ref[pl.ds(r0, CONV_ROWS), :] = (y * jax.nn.sigmoid(y)).astype(o_ref.dtype)
        return c

    lax.fori_loop(0, s // CONV_ROWS, tile, 0)


def _conv(u, conv_w, conv_b, g, b, batch, seq):
    n = u.shape[0]
    return pl.pallas_call(
        _conv_kernel,
        grid=(batch,),
        in_specs=[
            pl.BlockSpec((seq, 2 * D_CONV), lambda i: (i, 0)),
            pl.BlockSpec((CONV_WIDTH, D_CONV), lambda i: (0, 0)),
            pl.BlockSpec((1, D_CONV), lambda i: (0, 0)),
            pl.BlockSpec((1, D_CONV), lambda i: (0, 0)),
            pl.BlockSpec((1, D_CONV), lambda i: (0, 0)),
        ],
        out_specs=pl.BlockSpec((seq, D_CONV), lambda i: (i, 0)),
        out_shape=jax.ShapeDtypeStruct((n, D_CONV), BF16),
        scratch_shapes=[pltpu.VMEM((seq + CONV_PAD, D_CONV), F32)],
        compiler_params=pltpu.CompilerParams(
            dimension_semantics=("arbitrary",), vmem_limit_bytes=VMEM_LIMIT),
        name="conv",
    )(u, conv_w, conv_b.reshape(1, -1), g.reshape(1, -1), b.reshape(1, -1))


def _attn_kernel(slopes_ref, q_ref, k_ref, v_ref, o_ref, bias_ref, oacc_ref, m_ref, l_ref):
    blk = ATTN_BLOCK
    s_len = q_ref.shape[0]
    hp = pl.program_id(1)
    n_pat = len(DILATED_PATTERNS)

    lane = lax.broadcasted_iota(I32, (blk, LANES), 1)
    head0 = lane < HEAD_DIM
    lane2 = lax.broadcasted_iota(I32, (2 * blk, LANES), 1)
    head0_2 = lane2 < HEAD_DIM

    qi = lax.broadcasted_iota(I32, (blk, 2 * blk), 0)
    kj = lax.broadcasted_iota(I32, (blk, 2 * blk), 1)
    dist = qi + blk - kj
    is_prev = (kj < blk).astype(F32)
    for p, (window, dil) in enumerate(DILATED_PATTERNS):
        n_back = window // dil
        valid = (dist >= 0) & (dist <= n_back)
        dpos = (dist * dil).astype(F32)
        for hh in range(2):
            slope = slopes_ref[2 * hp + hh]
            bias_ref[p, hh] = jnp.where(valid, -slope * dpos, NEG_INF)

    def unit(p, dil, cur, prev, first):
        def rows(ref, start):
            if dil == 1:
                return ref[pl.ds(start, blk), :]
            return ref[pl.ds(start, blk, stride=dil), :]

        qb = rows(q_ref, cur) * (HEAD_DIM ** -0.5)
        if prev is None:
            kc = rows(k_ref, cur)
            vc = rows(v_ref, cur)
            hmask = head0
        else:
            kc = jnp.concatenate([rows(k_ref, prev), rows(k_ref, cur)], axis=0)
            vc = jnp.concatenate([rows(v_ref, prev), rows(v_ref, cur)], axis=0)
            hmask = head0_2
            pen = jnp.where(first, NEG_INF, 0.0) * is_prev
        kb = kc.astype(BF16)
        o_full = None
        m_full = None
        l_full = None
        for hh in range(2):
            sel = head0 if hh == 0 else jnp.logical_not(head0)
            vsel = hmask if hh == 0 else jnp.logical_not(hmask)
            qh = jnp.where(sel, qb, 0.0).astype(BF16)
            vh = jnp.where(vsel, vc, 0.0).astype(BF16)
            sc = lax.dot_general(qh, kb, (((1,), (1,)), ((), ())), preferred_element_type=F32)
            if prev is None:
                sc = sc + bias_ref[p, hh, :, blk:2 * blk]
            else:
                sc = sc + bias_ref[p, hh] + pen
            m = jnp.max(sc, axis=1, keepdims=True)
            e = jnp.exp(sc - m)
            l = jnp.sum(e, axis=1, keepdims=True)
            o = jnp.dot(e.astype(BF16), vh, preferred_element_type=F32)
            mb = jnp.broadcast_to(m, (blk, LANES))
            lb = jnp.broadcast_to(l, (blk, LANES))
            if hh == 0:
                o_full, m_full, l_full = o, mb, lb
            else:
                o_full = o_full + o
                m_full = jnp.where(head0, m_full, mb)
                l_full = jnp.where(head0, l_full, lb)
        if dil == 1:
            idx = pl.ds(cur, blk)
        else:
            idx = pl.ds(cur, blk, stride=dil)
        oacc_ref[p, idx, :] = o_full
        m_ref[p, idx, :] = m_full
        l_ref[p, idx, :] = l_full

    for p, (window, dil) in enumerate(DILATED_PATTERNS):
        stream_len = s_len // dil
        nb = stream_len // blk

        if nb == 1:
            def body(r, c, p=p, dil=dil):
                unit(p, dil, r, None, None)
                return c
            lax.fori_loop(0, dil, body, 0)
        else:
            def body(u, c, p=p, dil=dil, nb=nb):
                r = u // nb
                i = u - r * nb
                cur = r + i * (blk * dil)
                prev = r + jnp.maximum(i - 1, 0) * (blk * dil)
                if dil == 1:
                    cur = pl.multiple_of(cur, blk)
                    prev = pl.multiple_of(prev, blk)
                unit(p, dil, cur, prev, i == 0)
                return c
            lax.fori_loop(0, dil * nb, body, 0)

    def merge(i, c):
        r0 = pl.multiple_of(i * 256, 256)
        ms = [m_ref[p, pl.ds(r0, 256), :] for p in range(n_pat)]
        mx = functools.reduce(jnp.maximum, ms)
        num = None
        den = None
        for p in range(n_pat):
            w = jnp.exp(ms[p] - mx)
            tn = w * oacc_ref[p, pl.ds(r0, 256), :]
            td = w * l_ref[p, pl.ds(r0, 256), :]
            num = tn if num is None else num + tn
            den = td if den is None else den + td
        o_ref[pl.ds(r0, 256), :] = (num / den).astype(o_ref.dtype)
        return c

    lax.fori_loop(0, s_len // 256, merge, 0)


def _attention(qkv, slopes, batch, seq):
    n = qkv.shape[0]
    n_hp = D_ATTN // LANES
    n_pat = len(DILATED_PATTERNS)
    return pl.pallas_call(
        _attn_kernel,
        grid=(batch, n_hp),
        in_specs=[
            pl.BlockSpec(memory_space=pltpu.SMEM),
            pl.BlockSpec((seq, LANES), lambda b, h: (b, h)),
            pl.BlockSpec((seq, LANES), lambda b, h: (b, n_hp + h)),
            pl.BlockSpec((seq, LANES), lambda b, h: (b, 2 * n_hp + h)),
        ],
        out_specs=pl.BlockSpec((seq, LANES), lambda b, h: (b, h)),
        out_shape=jax.ShapeDtypeStruct((n, D_ATTN), BF16),
        scratch_shapes=[
            pltpu.VMEM((n_pat, 2, ATTN_BLOCK, 2 * ATTN_BLOCK), F32),
            pltpu.VMEM((n_pat, seq, LANES), F32),
            pltpu.VMEM((n_pat, seq, LANES), F32),
            pltpu.VMEM((n_pat, seq, LANES), F32),
        ],
        compiler_params=pltpu.CompilerParams(
            dimension_semantics=("arbitrary", "arbitrary"), vmem_limit_bytes=VMEM_LIMIT),
        name="attn",
    )(slopes, qkv, qkv, qkv)


def _out_proj_kernel(alpha, yc_ref, ya_ref, x_ref, wc_ref, wa_ref, b_ref, g_ref, be_ref, o_ref):
    y = jnp.dot(yc_ref[...], wc_ref[...], preferred_element_type=F32)
    y = y + jnp.dot(ya_ref[...], wa_ref[...], preferred_element_type=F32)
    y = y + b_ref[...]
    o_ref[...] = _ln(alpha * x_ref[...] + y, g_ref[...], be_ref[...])


def _out_proj(yc, ya, x2, w_out_bf, b_out, g, b, alpha, tm=256):
    n, d = x2.shape
    row = lambda i: (i, 0)
    fix = lambda i: (0, 0)
    return pl.pallas_call(
        functools.partial(_out_proj_kernel, alpha),
        grid=(n // tm,),
        in_specs=[
            pl.BlockSpec((tm, D_CONV), row),
            pl.BlockSpec((tm, D_ATTN), row),
            pl.BlockSpec((tm, d), row),
            pl.BlockSpec((D_CONV, d), fix),
            pl.BlockSpec((D_ATTN, d), fix),
            pl.BlockSpec((1, d), fix),
            pl.BlockSpec((1, d), fix),
            pl.BlockSpec((1, d), fix),
        ],
        out_specs=pl.BlockSpec((tm, d), row),
        out_shape=jax.ShapeDtypeStruct((n, d), F32),
        compiler_params=pltpu.CompilerParams(
            dimension_semantics=("arbitrary",), vmem_limit_bytes=VMEM_LIMIT),
        name="out_proj",
    )(yc, ya, x2, w_out_bf[:D_CONV], w_out_bf[D_CONV:], b_out.reshape(1, d),
      g.reshape(1, d), b.reshape(1, d))


ROUTE_TOKENS = 256


def _route_kernel(h_ref, wq_ref, keys_ref, idx_ref, g_ref, q_scr, gt_scr, it_scr):
    t = h_ref.shape[0]
    k_top = PEER_TOPK
    nk = PEER_NKEYS
    q_scr[...] = jnp.dot(h_ref[...].astype(BF16), wq_ref[...], preferred_element_type=F32)

    iota_n = lax.broadcasted_iota(I32, (nk, t), 0)
    iota_k = lax.broadcasted_iota(I32, (k_top, t), 0)
    pos = lax.broadcasted_iota(I32, (k_top * k_top, t), 0)
    neg = -jnp.inf

    def head(h, c):
        top_s = []
        top_i = []
        for p in range(2):
            col = pl.multiple_of(h * PEER_QDIM + p * (PEER_QDIM // 2), PEER_QDIM // 2)
            q = q_scr[:, pl.ds(col, PEER_QDIM // 2)].astype(BF16)
            s = lax.dot_general(keys_ref[h, p], q, (((1,), (1,)), ((), ())),
                                preferred_element_type=F32)
            ts = jnp.zeros((k_top, t), F32)
            ti = jnp.zeros((k_top, t), I32)
            for k in range(k_top):
                m = jnp.max(s, axis=0, keepdims=True)
                am = jnp.min(jnp.where(s == m, iota_n, nk), axis=0, keepdims=True)
                s = jnp.where(iota_n == am, neg, s)
                ts = jnp.where(iota_k == k, m, ts)
                ti = jnp.where(iota_k == k, am, ti)
            top_s.append(ts)
            top_i.append(ti)
        cand = jnp.concatenate(
            [top_s[0][i:i + 1, :] + top_s[1] for i in range(k_top)], axis=0)
        eid = jnp.concatenate(
            [top_i[0][i:i + 1, :] * nk + top_i[1] for i in range(k_top)], axis=0)
        bs = jnp.zeros((k_top, t), F32)
        be = jnp.zeros((k_top, t), I32)
        for k in range(k_top):
            m = jnp.max(cand, axis=0, keepdims=True)
            am = jnp.min(jnp.where(cand == m, pos, k_top * k_top), axis=0, keepdims=True)
            sel = pos == am
            e = jnp.max(jnp.where(sel, eid, -1), axis=0, keepdims=True)
            cand = jnp.where(sel, neg, cand)
            bs = jnp.where(iota_k == k, m, bs)
            be = jnp.where(iota_k == k, e, be)
        ex = jnp.exp(bs - bs[0:1, :])
        gate = ex / jnp.sum(ex, axis=0, keepdims=True)
        r0 = pl.multiple_of(h * k_top, k_top)
        gt_scr[pl.ds(r0, k_top), :] = gate
        it_scr[pl.ds(r0, k_top), :] = be
        return c

    lax.fori_loop(0, PEER_HEADS, head, 0)
    g_ref[...] = gt_scr[...].T
    idx_ref[...] = it_scr[...].T


def _route(h1, wq_bf, keys_bf, tm=ROUTE_TOKENS):
    n, d = h1.shape
    hk = PEER_HEADS * PEER_TOPK
    return pl.pallas_call(
        _route_kernel,
        grid=(n // tm,),
        in_specs=[
            pl.BlockSpec((tm, d), lambda i: (i, 0)),
            pl.BlockSpec(wq_bf.shape, lambda i: (0, 0)),
            pl.BlockSpec(keys_bf.shape, lambda i: (0, 0, 0, 0)),
        ],
        out_specs=[
            pl.BlockSpec((tm, hk), lambda i: (i, 0)),
            pl.BlockSpec((tm, hk), lambda i: (i, 0)),
        ],
        out_shape=[
            jax.ShapeDtypeStruct((n, hk), I32),
            jax.ShapeDtypeStruct((n, hk), F32),
        ],
        scratch_shapes=[
            pltpu.VMEM((tm, PEER_HEADS * PEER_QDIM), F32),
            pltpu.VMEM((hk, tm), F32),
            pltpu.VMEM((hk, tm), I32),
        ],
        compiler_params=pltpu.CompilerParams(
            dimension_semantics=("arbitrary",), vmem_limit_bytes=VMEM_LIMIT),
        name="route",
    )(h1, wq_bf, keys_bf)


def _peer_kernel(alpha, idx0_ref, idxa_ref, h_ref, g_ref, uv_ref, lg_ref, lb_ref, o_ref, *scratch):
    bufs = scratch[:PEER_ALLOCS]
    sem = scratch[PEER_ALLOCS]
    y_scr = scratch[PEER_ALLOCS + 1]
    tb, d = h_ref.shape
    hk = g_ref.shape[1]
    n_lt = d // LANES
    n_kg = hk // SUBLANES
    step = pl.program_id(0)

    def issue(idx_ref, tok, par, slot):
        for k in range(hk):
            e = idx_ref[tok * hk + k]
            pltpu.make_async_copy(
                uv_ref.at[e], bufs[par].at[slot, k // SUBLANES, :, k % SUBLANES, :],
                sem.at[par, slot]).start(priority=k % 2)

    def wait(par, slot):
        pltpu.make_async_copy(bufs[par].at[slot], bufs[par].at[slot], sem.at[par, slot]).wait()

    eye = (lax.broadcasted_iota(I32, (hk, hk), 0) == lax.broadcasted_iota(I32, (hk, hk), 1))
    ones_row = jnp.ones((SUBLANES, hk), BF16)
    ones_sq = jnp.ones((3 * hk, LANES), BF16)

    def compute(tok, par, slot):
        buf = bufs[par]
        x_row = h_ref[pl.ds(tok, 1), :]
        acc = None
        for c in range(n_lt):
            xc = x_row[:, c * LANES:(c + 1) * LANES].reshape(1, 1, LANES)
            term = buf[slot, :, c, :, :] * xc
            acc = term if acc is None else acc + term
        acc = acc.reshape(hk, LANES)
        hi = acc.astype(BF16)
        lo = (acc - hi.astype(F32)).astype(BF16)
        nt = (((1,), (1,)), ((), ()))
        a = (lax.dot_general(ones_row, hi, nt, preferred_element_type=F32)
             + lax.dot_general(ones_row, lo, nt, preferred_element_type=F32))
        act = 0.5 * a * (1.0 + lax.erf(a * (2.0 ** -0.5)))
        w = act * g_ref[pl.ds(tok, 1), :]
        w_hi = w.astype(BF16).astype(F32)
        r1 = w - w_hi
        w_mid = r1.astype(BF16).astype(F32)
        w_lo = r1 - w_mid
        diag = jnp.concatenate(
            [jnp.where(eye, jnp.broadcast_to(part[0:1, :], (hk, hk)), 0.0).astype(BF16)
             for part in (w_hi, w_mid, w_lo)], axis=1)
        wb = jnp.dot(diag, ones_sq, preferred_element_type=F32)
        wb = wb.reshape(n_kg, SUBLANES, LANES)
        return [jnp.sum(jnp.sum(buf[slot, :, n_lt + c, :, :] * wb, axis=0), axis=0, keepdims=True)
                for c in range(n_lt)]

    st = PEER_SET
    ahead_sets = PEER_AHEAD // st

    @pl.when(step == 0)
    def _():
        for t in range(PEER_AHEAD):
            issue(idx0_ref, t, (t // st) % PEER_ALLOCS, t % st)

    sub = lax.broadcasted_iota(I32, (SUBLANES, LANES), 0)
    sets_per_tile = SUBLANES // st
    body_tokens = PEER_ALLOCS * st

    def body(it, c):
        base = pl.multiple_of(it * body_tokens, body_tokens)
        y8 = None
        for q in range(PEER_ALLOCS):
            for j in range(st):
                wait(q, j)
            for j in range(st):
                issue(idxa_ref, base + q * st + j, (q + ahead_sets) % PEER_ALLOCS, j)
            if q % sets_per_tile == 0:
                y8 = [jnp.zeros((SUBLANES, LANES), F32) for _ in range(n_lt)]
            for j in range(st):
                rows = compute(base + q * st + j, q, j)
                r = (q % sets_per_tile) * st + j
                y8 = [jnp.where(sub == r, jnp.broadcast_to(rows[cc], (SUBLANES, LANES)), y8[cc])
                      for cc in range(n_lt)]
            if q % sets_per_tile == sets_per_tile - 1:
                r0 = pl.multiple_of(base + (q // sets_per_tile) * SUBLANES, SUBLANES)
                for cc in range(n_lt):
                    y_scr[pl.ds(r0, SUBLANES), cc * LANES:(cc + 1) * LANES] = y8[cc]
        return c

    lax.fori_loop(0, tb // body_tokens, body, 0)

    @pl.when(step == pl.num_programs(0) - 1)
    def _():
        for t in range(PEER_AHEAD):
            wait((t // st) % PEER_ALLOCS, t % st)

    o_ref[...] = _ln(alpha * h_ref[...] + y_scr[...], lg_ref[...], lb_ref[...])


def _peer(idx, h1, gates, uv3, g, b, alpha, tb=PEER_TOKENS):
    n, d = h1.shape
    hk = gates.shape[1]
    assert SUBLANES % PEER_SET == 0 and PEER_AHEAD % PEER_SET == 0
    assert 0 < PEER_AHEAD // PEER_SET < PEER_ALLOCS - 1
    assert tb % (PEER_ALLOCS * PEER_SET) == 0 and n % tb == 0
    idx_ahead = jnp.concatenate(
        [idx[PEER_AHEAD:], jnp.broadcast_to(idx[-1:], (PEER_AHEAD, hk))], axis=0).reshape(-1)
    idx_first = idx[:PEER_AHEAD].reshape(-1)
    row = lambda i: (i, 0)
    fix = lambda i: (0, 0)
    slot_shape = (PEER_SET, hk // SUBLANES, uv3.shape[1], SUBLANES, LANES)
    return pl.pallas_call(
        functools.partial(_peer_kernel, alpha),
        grid=(n // tb,),
        in_specs=[
            pl.BlockSpec(memory_space=pltpu.SMEM),
            pl.BlockSpec((tb * hk,), lambda i: (i,), memory_space=pltpu.SMEM),
            pl.BlockSpec((tb, d), row),
            pl.BlockSpec((tb, hk), row),
            pl.BlockSpec(memory_space=pl.ANY),
            pl.BlockSpec((1, d), fix),
            pl.BlockSpec((1, d), fix),
        ],
        out_specs=pl.BlockSpec((tb, d), row),
        out_shape=jax.ShapeDtypeStruct((n, d), F32),
        scratch_shapes=(
            [pltpu.VMEM(slot_shape, F32) for _ in range(PEER_ALLOCS)]
            + [pltpu.SemaphoreType.DMA((PEER_ALLOCS, PEER_SET)), pltpu.VMEM((tb, d), F32)]),
        compiler_params=pltpu.CompilerParams(
            dimension_semantics=("arbitrary",), vmem_limit_bytes=VMEM_LIMIT),
        name="peer",
    )(idx_first, idx_ahead, h1, gates, uv3, g.reshape(1, d), b.reshape(1, d))


def kernel(x, w_in, b_in, conv_w, conv_b, conv_ln_g, conv_ln_b, w_out, b_out, ln1_g, ln1_b,
           peer_wq, peer_keys, peer_u, peer_v, ln2_g, ln2_b):
    batch, seq, d = x.shape
    depth = w_in.shape[0]
    alpha = (2.0 * depth) ** 0.25
    slopes = 2.0 ** (-8.0 * jnp.arange(1, N_HEADS + 1, dtype=F32) / N_HEADS)
    h = x.reshape(batch * seq, d)
    for l in range(depth):
        u, qkv = _in_proj(h, w_in[l].astype(BF16), b_in[l])
        yc = _conv(u, conv_w[l], conv_b[l], conv_ln_g[l], conv_ln_b[l], batch, seq)
        ya = _attention(qkv, slopes, batch, seq)
        h1 = _out_proj(yc, ya, h, w_out[l].astype(BF16), b_out[l], ln1_g[l], ln1_b[l], alpha)
        idx, gates = _route(h1, peer_wq[l].astype(BF16), peer_keys[l].astype(BF16))
        uv3 = jnp.concatenate([peer_u[l], peer_v[l]], axis=1).reshape(-1, 2 * d // LANES, LANES)
        h = _peer(idx, h1, gates, uv3, ln2_g[l], ln2_b[l], alpha)
    return h.reshape(batch, seq, d)
```

```python
import functools
import math

import jax
import jax.numpy as jnp
from jax import lax
from jax.experimental import pallas as pl
from jax.experimental.pallas import tpu as pltpu

F32 = jnp.float32
BF16 = jnp.bfloat16
I32 = jnp.int32

D_MODEL = 1024
D_CONV = D_MODEL // 2
D_ATTN = D_MODEL - D_CONV
HEAD_DIM = 64
N_HEADS = D_ATTN // HEAD_DIM
CONV_WIDTH = 31
DILATED_PATTERNS = ((128, 1), (512, 4), (2048, 16))
ATTN_BLOCK = 128
PEER_HEADS = 8
PEER_NKEYS = 128
PEER_QDIM = 256
PEER_TOPK = 16
LN_EPS = 1e-5
NEG_INF = -1e30

LANES = 128
SUBLANES = 8
VMEM_LIMIT = 56 * 1024 * 1024

PEER_SET = 8
PEER_ALLOCS = 4
PEER_AHEAD = 2 * PEER_SET
PEER_TOKENS = 128


def _ln(x, g, b):
    mu = jnp.mean(x, axis=-1, keepdims=True)
    xc = x - mu
    var = jnp.mean(xc * xc, axis=-1, keepdims=True)
    return xc * lax.rsqrt(var + LN_EPS) * g + b


def _in_proj_kernel(x_ref, w_ref, b_ref, u_ref, qkv_ref):
    x = x_ref[...].astype(BF16)
    nu = u_ref.shape[1]
    u_ref[...] = jnp.dot(x, w_ref[:, :nu], preferred_element_type=F32) + b_ref[:, :nu]
    qkv_ref[...] = jnp.dot(x, w_ref[:, nu:], preferred_element_type=F32) + b_ref[:, nu:]


def _in_proj(x2, w_bf, b, tm=256):
    n, d = x2.shape
    dout = w_bf.shape[1]
    nu = 2 * D_CONV
    return pl.pallas_call(
        _in_proj_kernel,
        grid=(n // tm,),
        in_specs=[
            pl.BlockSpec((tm, d), lambda i: (i, 0)),
            pl.BlockSpec((d, dout), lambda i: (0, 0)),
            pl.BlockSpec((1, dout), lambda i: (0, 0)),
        ],
        out_specs=[
            pl.BlockSpec((tm, nu), lambda i: (i, 0)),
            pl.BlockSpec((tm, dout - nu), lambda i: (i, 0)),
        ],
        out_shape=[
            jax.ShapeDtypeStruct((n, nu), F32),
            jax.ShapeDtypeStruct((n, dout - nu), F32),
        ],
        compiler_params=pltpu.CompilerParams(
            dimension_semantics=("arbitrary",), vmem_limit_bytes=VMEM_LIMIT),
        name="in_proj",
    )(x2, w_bf, b.reshape(1, dout))


CONV_ROWS = 64
CONV_PAD = 32


def _conv_kernel(u_ref, w_ref, cb_ref, g_ref, b_ref, o_ref, pad_ref):
    s = o_ref.shape[0]
    pad_ref[0:CONV_PAD, :] = jnp.zeros((CONV_PAD, D_CONV), F32)

    def glu(i, c):
        r0 = pl.multiple_of(i * 256, 256)
        a = u_ref[pl.ds(r0, 256), 0:D_CONV]
        gate = u_ref[pl.ds(r0, 256), D_CONV:2 * D_CONV]
        pad_ref[pl.ds(CONV_PAD + r0, 256), :] = a * jax.nn.sigmoid(gate)
        return c

    lax.fori_loop(0, s // 256, glu, 0)

    off = CONV_PAD - (CONV_WIDTH - 1)

    def tile(i, c):
        r0 = pl.multiple_of(i * CONV_ROWS, CONV_ROWS)
        acc = jnp.broadcast_to(cb_ref[...], (CONV_ROWS, D_CONV))
        win = pad_ref[pl.ds(r0, CONV_ROWS + CONV_PAD), :]
        for j in range(CONV_WIDTH):
            acc = acc + w_ref[j:j + 1, :] * win[off + j:off + j + CONV_ROWS, :]
        y = _ln(acc, g_ref[...], b_ref[...])
        o_ref[pl.ds(r0, CONV_ROWS), :] = (y * jax.nn.sigmoid(y)).astype(o_ref.dtype)
        return c

    lax.fori_loop(0, s // CONV_ROWS, tile, 0)


def _conv(u, conv_w, conv_b, g, b, batch, seq):
    n = u.shape[0]
    return pl.pallas_call(
        _conv_kernel,
        grid=(batch,),
        in_specs=[
            pl.BlockSpec((seq, 2 * D_CONV), lambda i: (i, 0)),
            pl.BlockSpec((CONV_WIDTH, D_CONV), lambda i: (0, 0)),
            pl.BlockSpec((1, D_CONV), lambda i: (0, 0)),
            pl.BlockSpec((1, D_CONV), lambda i: (0, 0)),
            pl.BlockSpec((1, D_CONV), lambda i: (0, 0)),
        ],
        out_specs=pl.BlockSpec((seq, D_CONV), lambda i: (i, 0)),
        out_shape=jax.ShapeDtypeStruct((n, D_CONV), BF16),
        scratch_shapes=[pltpu.VMEM((seq + CONV_PAD, D_CONV), F32)],
        compiler_params=pltpu.CompilerParams(
            dimension_semantics=("arbitrary",), vmem_limit_bytes=VMEM_LIMIT),
        name="conv",
    )(u, conv_w, conv_b.reshape(1, -1), g.reshape(1, -1), b.reshape(1, -1))


def _attn_kernel(slopes_ref, q_ref, k_ref, v_ref, o_ref, bias_ref, oacc_ref, m_ref, l_ref):
    blk = ATTN_BLOCK
    s_len = q_ref.shape[0]
    hp = pl.program_id(1)
    n_pat = len(DILATED_PATTERNS)

    lane = lax.broadcasted_iota(I32, (blk, LANES), 1)
    head0 = lane < HEAD_DIM
    lane2 = lax.broadcasted_iota(I32, (2 * blk, LANES), 1)
    head0_2 = lane2 < HEAD_DIM

    qi = lax.broadcasted_iota(I32, (blk, 2 * blk), 0)
    kj = lax.broadcasted_iota(I32, (blk, 2 * blk), 1)
    dist = qi + blk - kj
    is_prev = (kj < blk).astype(F32)
    for p, (window, dil) in enumerate(DILATED_PATTERNS):
        n_back = window // dil
        valid = (dist >= 0) & (dist <= n_back)
        dpos = (dist * dil).astype(F32)
        for hh in range(2):
            slope = slopes_ref[2 * hp + hh]
            bias_ref[p, hh] = jnp.where(valid, -slope * dpos, NEG_INF)

    def unit(p, dil, cur, prev, first):
        def rows(ref, start):
            if dil == 1:
                return ref[pl.ds(start, blk), :]
            return ref[pl.ds(start, blk, stride=dil), :]

        qb = rows(q_ref, cur) * (HEAD_DIM ** -0.5)
        if prev is None:
            kc = rows(k_ref, cur)
            vc = rows(v_ref, cur)
            hmask = head0
        else:
            kc = jnp.concatenate([rows(k_ref, prev), rows(k_ref, cur)], axis=0)
            vc = jnp.concatenate([rows(v_ref, prev), rows(v_ref, cur)], axis=0)
            hmask = head0_2
            pen = jnp.where(first, NEG_INF, 0.0) * is_prev
        kb = kc.astype(BF16)
        o_full = None
        m_full = None
        l_full = None
        for hh in range(2):
            sel = head0 if hh == 0 else jnp.logical_not(head0)
            vsel = hmask if hh == 0 else jnp.logical_not(hmask)
            qh = jnp.where(sel, qb, 0.0).astype(BF16)
            vh = jnp.where(vsel, vc, 0.0).astype(BF16)
            sc = lax.dot_general(qh, kb, (((1,), (1,)), ((), ())), preferred_element_type=F32)
            if prev is None:
                sc = sc + bias_ref[p, hh, :, blk:2 * blk]
            else:
                sc = sc + bias_ref[p, hh] + pen
            m = jnp.max(sc, axis=1, keepdims=True)
            e = jnp.exp(sc - m)
            l = jnp.sum(e, axis=1, keepdims=True)
            o = jnp.dot(e.astype(BF16), vh, preferred_element_type=F32)
            mb = jnp.broadcast_to(m, (blk, LANES))
            lb = jnp.broadcast_to(l, (blk, LANES))
            if hh == 0:
                o_full, m_full, l_full = o, mb, lb
            else:
                o_full = o_full + o
                m_full = jnp.where(head0, m_full, mb)
                l_full = jnp.where(head0, l_full, lb)
        if dil == 1:
            idx = pl.ds(cur, blk)
        else:
            idx = pl.ds(cur, blk, stride=dil)
        oacc_ref[p, idx, :] = o_full
        m_ref[p, idx, :] = m_full
        l_ref[p, idx, :] = l_full

    for p, (window, dil) in enumerate(DILATED_PATTERNS):
        stream_len = s_len // dil
        nb = stream_len // blk

        if nb == 1:
            def body(r, c, p=p, dil=dil):
                unit(p, dil, r, None, None)
                return c
            lax.fori_loop(0, dil, body, 0)
        else:
            def body(u, c, p=p, dil=dil, nb=nb):
                r = u // nb
                i = u - r * nb
                cur = r + i * (blk * dil)
                prev = r + jnp.maximum(i - 1, 0) * (blk * dil)
                if dil == 1:
                    cur = pl.multiple_of(cur, blk)
                    prev = pl.multiple_of(prev, blk)
                unit(p, dil, cur, prev, i == 0)
                return c
            lax.fori_loop(0, dil * nb, body, 0)

    def merge(i, c):
        r0 = pl.multiple_of(i * 256, 256)
        ms = [m_ref[p, pl.ds(r0, 256), :] for p in range(n_pat)]
        mx = functools.reduce(jnp.maximum, ms)
        num = None
        den = None
        for p in range(n_pat):
            w = jnp.exp(ms[p] - mx)
            tn = w * oacc_ref[p, pl.ds(r0, 256), :]
            td = w * l_ref[p, pl.ds(r0, 256), :]
            num = tn if num is None else num + tn
            den = td if den is None else den + td
        o_ref[pl.ds(r0, 256), :] = (num / den).astype(o_ref.dtype)
        return c

    lax.fori_loop(0, s_len // 256, merge, 0)


def _attention(qkv, slopes, batch, seq):
    n = qkv.shape[0]
    n_hp = D_ATTN // LANES
    n_pat = len(DILATED_PATTERNS)
    return pl.pallas_call(
        _attn_kernel,
        grid=(batch, n_hp),
        in_specs=[
            pl.BlockSpec(memory_space=pltpu.SMEM),
            pl.BlockSpec((seq, LANES), lambda b, h: (b, h)),
            pl.BlockSpec((seq, LANES), lambda b, h: (b, n_hp + h)),
            pl.BlockSpec((seq, LANES), lambda b, h: (b, 2 * n_hp + h)),
        ],
        out_specs=pl.BlockSpec((seq, LANES), lambda b, h: (b, h)),
        out_shape=jax.ShapeDtypeStruct((n, D_ATTN), BF16),
        scratch_shapes=[
            pltpu.VMEM((n_pat, 2, ATTN_BLOCK, 2 * ATTN_BLOCK), F32),
            pltpu.VMEM((n_pat, seq, LANES), F32),
            pltpu.VMEM((n_pat, seq, LANES), F32),
            pltpu.VMEM((n_pat, seq, LANES), F32),
        ],
        compiler_params=pltpu.CompilerParams(
            dimension_semantics=("arbitrary", "arbitrary"), vmem_limit_bytes=VMEM_LIMIT),
        name="attn",
    )(slopes, qkv, qkv, qkv)


def _out_proj_kernel(alpha, yc_ref, ya_ref, x_ref, wc_ref, wa_ref, b_ref, g_ref, be_ref, o_ref):
    y = jnp.dot(yc_ref[...], wc_ref[...], preferred_element_type=F32)
    y = y + jnp.dot(ya_ref[...], wa_ref[...], preferred_element_type=F32)
    y = y + b_ref[...]
    o_ref[...] = _ln(alpha * x_ref[...] + y, g_ref[...], be_ref[...])


def _out_proj(yc, ya, x2, w_out_bf, b_out, g, b, alpha, tm=256):
    n, d = x2.shape
    row = lambda i: (i, 0)
    fix = lambda i: (0, 0)
    return pl.pallas_call(
        functools.partial(_out_proj_kernel, alpha),
        grid=(n // tm,),
        in_specs=[
            pl.BlockSpec((tm, D_CONV), row),
            pl.BlockSpec((tm, D_ATTN), row),
            pl.BlockSpec((tm, d), row),
            pl.BlockSpec((D_CONV, d), fix),
            pl.BlockSpec((D_ATTN, d), fix),
            pl.BlockSpec((1, d), fix),
            pl.BlockSpec((1, d), fix),
            pl.BlockSpec((1, d), fix),
        ],
        out_specs=pl.BlockSpec((tm, d), row),
        out_shape=jax.ShapeDtypeStruct((n, d), F32),
        compiler_params=pltpu.CompilerParams(
            dimension_semantics=("arbitrary",), vmem_limit_bytes=VMEM_LIMIT),
        name="out_proj",
    )(yc, ya, x2, w_out_bf[:D_CONV], w_out_bf[D_CONV:], b_out.reshape(1, d),
      g.reshape(1, d), b.reshape(1, d))


ROUTE_TOKENS = 256


def _route_kernel(h_ref, wq_ref, keys_ref, idx_ref, g_ref, q_scr, gt_scr, it_scr):
    t = h_ref.shape[0]
    k_top = PEER_TOPK
    nk = PEER_NKEYS
    q_scr[...] = jnp.dot(h_ref[...].astype(BF16), wq_ref[...], preferred_element_type=F32)

    iota_n = lax.broadcasted_iota(I32, (nk, t), 0)
    iota_k = lax.broadcasted_iota(I32, (k_top, t), 0)
    pos = lax.broadcasted_iota(I32, (k_top * k_top, t), 0)
    neg = -jnp.inf

    def head(h, c):
        top_s = []
        top_i = []
        for p in range(2):
            col = pl.multiple_of(h * PEER_QDIM + p * (PEER_QDIM // 2), PEER_QDIM // 2)
            q = q_scr[:, pl.ds(col, PEER_QDIM // 2)].astype(BF16)
            s = lax.dot_general(keys_ref[h, p], q, (((1,), (1,)), ((), ())),
                                preferred_element_type=F32)
            ts = jnp.zeros((k_top, t), F32)
            ti = jnp.zeros((k_top, t), I32)
            for k in range(k_top):
                m = jnp.max(s, axis=0, keepdims=True)
                am = jnp.min(jnp.where(s == m, iota_n, nk), axis=0, keepdims=True)
                s = jnp.where(iota_n == am, neg, s)
                ts = jnp.where(iota_k == k, m, ts)
                ti = jnp.where(iota_k == k, am, ti)
            top_s.append(ts)
            top_i.append(ti)
        cand = jnp.concatenate(
            [top_s[0][i:i + 1, :] + top_s[1] for i in range(k_top)], axis=0)
        eid = jnp.concatenate(
            [top_i[0][i:i + 1, :] * nk + top_i[1] for i in range(k_top)], axis=0)
        bs = jnp.zeros((k_top, t), F32)
        be = jnp.zeros((k_top, t), I32)
        for k in range(k_top):
            m = jnp.max(cand, axis=0, keepdims=True)
            am = jnp.min(jnp.where(cand == m, pos, k_top * k_top), axis=0, keepdims=True)
            sel = pos == am
            e = jnp.max(jnp.where(sel, eid, -1), axis=0, keepdims=True)
            cand = jnp.where(sel, neg, cand)
            bs = jnp.where(iota_k == k, m, bs)
            be = jnp.where(iota_k == k, e, be)
        ex = jnp.exp(bs - bs[0:1, :])
        gate = ex / jnp.sum(ex, axis=0, keepdims=True)
        r0 = pl.multiple_of(h * k_top, k_top)
        gt_scr[pl.ds(r0, k_top), :] = gate
        it_scr[pl.ds(r0, k_top), :] = be
        return c

    lax.fori_loop(0, PEER_HEADS, head, 0)
    g_ref[...] = gt_scr[...].T
    idx_ref[...] = it_scr[...].T


def _route(h1, wq_bf, keys_bf, tm=ROUTE_TOKENS):
    n, d = h1.shape
    hk = PEER_HEADS * PEER_TOPK
    return pl.pallas_call(
        _route_kernel,
        grid=(n // tm,),
        in_specs=[
            pl.BlockSpec((tm, d), lambda i: (i, 0)),
            pl.BlockSpec(wq_bf.shape, lambda i: (0, 0)),
            pl.BlockSpec(keys_bf.shape, lambda i: (0, 0, 0, 0)),
        ],
        out_specs=[
            pl.BlockSpec((tm, hk), lambda i: (i, 0)),
            pl.BlockSpec((tm, hk), lambda i: (i, 0)),
        ],
        out_shape=[
            jax.ShapeDtypeStruct((n, hk), I32),
            jax.ShapeDtypeStruct((n, hk), F32),
        ],
        scratch_shapes=[
            pltpu.VMEM((tm, PEER_HEADS * PEER_QDIM), F32),
            pltpu.VMEM((hk, tm), F32),
            pltpu.VMEM((hk, tm), I32),
        ],
        compiler_params=pltpu.CompilerParams(
            dimension_semantics=("arbitrary",), vmem_limit_bytes=VMEM_LIMIT),
        name="route",
    )(h1, wq_bf, keys_bf)


def _peer_kernel(alpha, idx0_ref, idxa_ref, h_ref, g_ref, uv_ref, lg_ref, lb_ref, o_ref, *scratch):
    bufs = scratch[:PEER_ALLOCS]
    sem, y_scr, acc_scr, wb_scr = scratch[PEER_ALLOCS:]
    tb, d = h_ref.shape
    hk = g_ref.shape[1]
    n_lt = d // LANES
    n_kg = hk // SUBLANES
    st = PEER_SET
    ahead_sets = PEER_AHEAD // st
    step = pl.program_id(0)

    def issue_group(idx_ref, tok, alloc, slot, kg):
        for r in range(SUBLANES):
            e = idx_ref[tok * hk + kg * SUBLANES + r]
            pltpu.make_async_copy(
                uv_ref.at[e], bufs[alloc].at[slot, kg, :, r, :],
                sem.at[alloc, slot]).start(priority=r % 2)

    def wait(alloc, slot):
        pltpu.make_async_copy(
            bufs[alloc].at[slot], bufs[alloc].at[slot], sem.at[alloc, slot]).wait()

    eye = (lax.broadcasted_iota(I32, (hk, hk), 0) == lax.broadcasted_iota(I32, (hk, hk), 1))
    ones_row = jnp.ones((SUBLANES, LANES), BF16)
    ones_sq = jnp.ones((hk, LANES), BF16)
    sub = lax.broadcasted_iota(I32, (SUBLANES, LANES), 0)

    def gate_chain(tok0):
        acc = acc_scr[...].reshape(st * hk, LANES).astype(BF16)
        sums = lax.dot_general(ones_row, acc, (((1,), (1,)), ((), ())),
                               preferred_element_type=F32)
        a = jnp.zeros((st, hk), F32)
        for j in range(st):
            a = jnp.where(sub == j, sums[:, j * hk:(j + 1) * hk], a)
        act = 0.5 * a * (1.0 + lax.erf(a * (2.0 ** -0.5)))
        w = act * g_ref[pl.ds(tok0, st), :]
        diag = jnp.concatenate(
            [jnp.where(eye, jnp.broadcast_to(w[j:j + 1, :], (hk, hk)), 0.0).astype(BF16)
             for j in range(st)], axis=0)
        wb = jnp.dot(diag, ones_sq, preferred_element_type=F32)
        wb_scr[...] = wb.reshape(st, hk, LANES)

    @pl.when(step == 0)
    def _():
        for a in range(ahead_sets):
            def first(j, c, a=a):
                for kg in range(n_kg):
                    issue_group(idx0_ref, a * st + j, a, j, kg)
                return c
            lax.fori_loop(0, st, first, 0)

    body_tokens = PEER_ALLOCS * st

    def body(it, carry):
        base = pl.multiple_of(it * body_tokens, body_tokens)
        for q in range(PEER_ALLOCS):
            buf = bufs[q]
            nxt = (q + ahead_sets) % PEER_ALLOCS
            tok0 = base + q * st
            for j in range(st):
                wait(q, j)

            def loop_a(i, c, buf=buf, nxt=nxt, tok0=tok0):
                for j in range(st):
                    issue_group(idxa_ref, tok0 + j, nxt, j, i)
                for j in range(st):
                    x_row = h_ref[pl.ds(tok0 + j, 1), :]
                    for g2 in range(2):
                        kg = 2 * i + g2
                        acc = None
                        for c2 in range(n_lt):
                            xc = jnp.broadcast_to(x_row[:, c2 * LANES:(c2 + 1) * LANES],
                                                  (SUBLANES, LANES))
                            term = buf[j, kg, c2, :, :] * xc
                            acc = term if acc is None else acc + term
                        r0 = pl.multiple_of(kg * SUBLANES, SUBLANES)
                        acc_scr[j, pl.ds(r0, SUBLANES), :] = acc
                return c

            lax.fori_loop(0, n_kg // 2, loop_a, 0)

            gate_chain(pl.multiple_of(tok0, st))

            def loop_b(c2, c, buf=buf, nxt=nxt, tok0=tok0):
                for j in range(st):
                    issue_group(idxa_ref, tok0 + j, nxt, j, n_kg // 2 + c2)
                tile = jnp.zeros((SUBLANES, LANES), F32)
                for j in range(st):
                    wb = wb_scr[j].reshape(n_kg, SUBLANES, LANES)
                    prod = buf[j, :, n_lt + c2, :, :] * wb
                    row = jnp.sum(jnp.sum(prod, axis=0), axis=0, keepdims=True)
                    tile = jnp.where(sub == j, jnp.broadcast_to(row, (SUBLANES, LANES)), tile)
                col = pl.multiple_of(c2 * LANES, LANES)
                y_scr[pl.ds(pl.multiple_of(tok0, st), st), pl.ds(col, LANES)] = tile
                return c

            lax.fori_loop(0, n_lt, loop_b, 0)
        return carry

    lax.fori_loop(0, tb // body_tokens, body, 0)

    @pl.when(step == pl.num_programs(0) - 1)
    def _():
        for t in range(PEER_AHEAD):
            wait((t // st) % PEER_ALLOCS, t % st)

    o_ref[...] = _ln(alpha * h_ref[...] + y_scr[...], lg_ref[...], lb_ref[...])


def _peer(idx, h1, gates, uv3, g, b, alpha, tb=PEER_TOKENS):
    n, d = h1.shape
    hk = gates.shape[1]
    assert SUBLANES == PEER_SET and PEER_AHEAD % PEER_SET == 0
    assert 0 < PEER_AHEAD // PEER_SET < PEER_ALLOCS - 1
    assert hk // SUBLANES == 2 * (d // LANES)
    assert tb % (PEER_ALLOCS * PEER_SET) == 0 and n % tb == 0
    idx_ahead = jnp.concatenate(
        [idx[PEER_AHEAD:], jnp.broadcast_to(idx[-1:], (PEER_AHEAD, hk))], axis=0).reshape(-1)
    idx_first = idx[:PEER_AHEAD].reshape(-1)
    row = lambda i: (i, 0)
    fix = lambda i: (0, 0)
    slot_shape = (PEER_SET, hk // SUBLANES, uv3.shape[1], SUBLANES, LANES)
    return pl.pallas_call(
        functools.partial(_peer_kernel, alpha),
        grid=(n // tb,),
        in_specs=[
            pl.BlockSpec(memory_space=pltpu.SMEM),
            pl.BlockSpec((tb * hk,), lambda i: (i,), memory_space=pltpu.SMEM),
            pl.BlockSpec((tb, d), row),
            pl.BlockSpec((tb, hk), row),
            pl.BlockSpec(memory_space=pl.ANY),
            pl.BlockSpec((1, d), fix),
            pl.BlockSpec((1, d), fix),
        ],
        out_specs=pl.BlockSpec((tb, d), row),
        out_shape=jax.ShapeDtypeStruct((n, d), F32),
        scratch_shapes=(
            [pltpu.VMEM(slot_shape, F32) for _ in range(PEER_ALLOCS)]
            + [pltpu.SemaphoreType.DMA((PEER_ALLOCS, PEER_SET)),
               pltpu.VMEM((tb, d), F32),
               pltpu.VMEM((PEER_SET, hk, LANES), F32),
               pltpu.VMEM((PEER_SET, hk, LANES), F32)]),
        compiler_params=pltpu.CompilerParams(
            dimension_semantics=("arbitrary",), vmem_limit_bytes=VMEM_LIMIT),
        name="peer",
    )(idx_first, idx_ahead, h1, gates, uv3, g.reshape(1, d), b.reshape(1, d))


def kernel(x, w_in, b_in, conv_w, conv_b, conv_ln_g, conv_ln_b, w_out, b_out, ln1_g, ln1_b,
           peer_wq, peer_keys, peer_u, peer_v, ln2_g, ln2_b):
    batch, seq, d = x.shape
    depth = w_in.shape[0]
    alpha = (2.0 * depth) ** 0.25
    slopes = 2.0 ** (-8.0 * jnp.arange(1, N_HEADS + 1, dtype=F32) / N_HEADS)
    h = x.reshape(batch * seq, d)
    for l in range(depth):
        u, qkv = _in_proj(h, w_in[l].astype(BF16), b_in[l])
        yc = _conv(u, conv_w[l], conv_b[l], conv_ln_g[l], conv_ln_b[l], batch, seq)
        ya = _attention(qkv, slopes, batch, seq)
        h1 = _out_proj(yc, ya, h, w_out[l].astype(BF16), b_out[l], ln1_g[l], ln1_b[l], alpha)
        idx, gates = _route(h1, peer_wq[l].astype(BF16), peer_keys[l].astype(BF16))
        uv3 = jnp.concatenate([peer_u[l], peer_v[l]], axis=1).reshape(-1, 2 * d // LANES, LANES)
        h = _peer(idx, h1, gates, uv3, ln2_g[l], ln2_b[l], alpha)
    return h.reshape(batch, seq, d)
```

```python
import functools
import math

import jax
import jax.numpy as jnp
from jax import lax
from jax.experimental import pallas as pl
from jax.experimental.pallas import tpu as pltpu

F32 = jnp.float32
BF16 = jnp.bfloat16
I32 = jnp.int32

D_MODEL = 1024
D_CONV = D_MODEL // 2
D_ATTN = D_MODEL - D_CONV
HEAD_DIM = 64
N_HEADS = D_ATTN // HEAD_DIM
CONV_WIDTH = 31
DILATED_PATTERNS = ((128, 1), (512, 4), (2048, 16))
ATTN_BLOCK = 128
ATTN_UNROLL = 4
PEER_HEADS = 8
PEER_NKEYS = 128
PEER_QDIM = 256
PEER_TOPK = 16
LN_EPS = 1e-5
NEG_INF = -1e30

LANES = 128
SUBLANES = 8
VMEM_LIMIT = 56 * 1024 * 1024

PEER_SET = 8
PEER_ALLOCS = 4
PEER_AHEAD = 2 * PEER_SET
PEER_TOKENS = 128


def _ln(x, g, b):
    mu = jnp.mean(x, axis=-1, keepdims=True)
    xc = x - mu
    var = jnp.mean(xc * xc, axis=-1, keepdims=True)
    return xc * lax.rsqrt(var + LN_EPS) * g + b


def _in_proj_kernel(x_ref, w_ref, b_ref, u_ref, qkv_ref):
    x = x_ref[...].astype(BF16)
    nu = u_ref.shape[1]
    u_ref[...] = jnp.dot(x, w_ref[:, :nu], preferred_element_type=F32) + b_ref[:, :nu]
    qkv_ref[...] = jnp.dot(x, w_ref[:, nu:], preferred_element_type=F32) + b_ref[:, nu:]


def _in_proj(x2, w_bf, b, tm=256):
    n, d = x2.shape
    dout = w_bf.shape[1]
    nu = 2 * D_CONV
    return pl.pallas_call(
        _in_proj_kernel,
        grid=(n // tm,),
        in_specs=[
            pl.BlockSpec((tm, d), lambda i: (i, 0)),
            pl.BlockSpec((d, dout), lambda i: (0, 0)),
            pl.BlockSpec((1, dout), lambda i: (0, 0)),
        ],
        out_specs=[
            pl.BlockSpec((tm, nu), lambda i: (i, 0)),
            pl.BlockSpec((tm, dout - nu), lambda i: (i, 0)),
        ],
        out_shape=[
            jax.ShapeDtypeStruct((n, nu), F32),
            jax.ShapeDtypeStruct((n, dout - nu), F32),
        ],
        compiler_params=pltpu.CompilerParams(
            dimension_semantics=("arbitrary",), vmem_limit_bytes=VMEM_LIMIT),
        name="in_proj",
    )(x2, w_bf, b.reshape(1, dout))


CONV_ROWS = 64
CONV_PAD = 32


def _conv_kernel(u_ref, w_ref, cb_ref, g_ref, b_ref, o_ref, pad_ref):
    s = o_ref.shape[0]
    pad_ref[0:CONV_PAD, :] = jnp.zeros((CONV_PAD, D_CONV), F32)

    def glu(i, c):
        r0 = pl.multiple_of(i * 256, 256)
        a = u_ref[pl.ds(r0, 256), 0:D_CONV]
        gate = u_ref[pl.ds(r0, 256), D_CONV:2 * D_CONV]
        pad_ref[pl.ds(CONV_PAD + r0, 256), :] = a * jax.nn.sigmoid(gate)
        return c

    lax.fori_loop(0, s // 256, glu, 0)

    off = CONV_PAD - (CONV_WIDTH - 1)

    def tile(i, c):
        r0 = pl.multiple_of(i * CONV_ROWS, CONV_ROWS)
        acc = jnp.broadcast_to(cb_ref[...], (CONV_ROWS, D_CONV))
        win = pad_ref[pl.ds(r0, CONV_ROWS + CONV_PAD), :]
        for j in range(CONV_WIDTH):
            acc = acc + w_ref[j:j + 1, :] * win[off + j:off + j + CONV_ROWS, :]
        y = _ln(acc, g_ref[...], b_ref[...])
        o_ref[pl.ds(r0, CONV_ROWS), :] = (y * jax.nn.sigmoid(y)).astype(o_ref.dtype)
        return c

    lax.fori_loop(0, s // CONV_ROWS, tile, 0)


def _conv(u, conv_w, conv_b, g, b, batch, seq):
    n = u.shape[0]
    return pl.pallas_call(
        _conv_kernel,
        grid=(batch,),
        in_specs=[
            pl.BlockSpec((seq, 2 * D_CONV), lambda i: (i, 0)),
            pl.BlockSpec((CONV_WIDTH, D_CONV), lambda i: (0, 0)),
            pl.BlockSpec((1, D_CONV), lambda i: (0, 0)),
            pl.BlockSpec((1, D_CONV), lambda i: (0, 0)),
            pl.BlockSpec((1, D_CONV), lambda i: (0, 0)),
        ],
        out_specs=pl.BlockSpec((seq, D_CONV), lambda i: (i, 0)),
        out_shape=jax.ShapeDtypeStruct((n, D_CONV), BF16),
        scratch_shapes=[pltpu.VMEM((seq + CONV_PAD, D_CONV), F32)],
        compiler_params=pltpu.CompilerParams(
            dimension_semantics=("arbitrary",), vmem_limit_bytes=VMEM_LIMIT),
        name="conv",
    )(u, conv_w, conv_b.reshape(1, -1), g.reshape(1, -1), b.reshape(1, -1))


def _attn_kernel(slopes_ref, q_ref, k_ref, v_ref, o_ref, bias_ref, oacc_ref, m_ref, l_ref):
    blk = ATTN_BLOCK
    s_len = q_ref.shape[0]
    hp = pl.program_id(1)
    n_pat = len(DILATED_PATTERNS)

    lane = lax.broadcasted_iota(I32, (blk, LANES), 1)
    head0 = lane < HEAD_DIM
    lane2 = lax.broadcasted_iota(I32, (2 * blk, LANES), 1)
    head0_2 = lane2 < HEAD_DIM

    qi = lax.broadcasted_iota(I32, (blk, 2 * blk), 0)
    kj = lax.broadcasted_iota(I32, (blk, 2 * blk), 1)
    dist = qi + blk - kj
    is_prev = (kj < blk).astype(F32)
    for p, (window, dil) in enumerate(DILATED_PATTERNS):
        n_back = window // dil
        valid = (dist >= 0) & (dist <= n_back)
        dpos = (dist * dil).astype(F32)
        for hh in range(2):
            slope = slopes_ref[2 * hp + hh]
            bias_ref[p, hh] = jnp.where(valid, -slope * dpos, NEG_INF)

    def units(p, dil, specs):
        def rows(ref, start):
            if dil == 1:
                return ref[pl.ds(start, blk), :]
            return ref[pl.ds(start, blk, stride=dil), :]

        scores = []
        values = []
        for cur, prev, first in specs:
            qb = rows(q_ref, cur) * (HEAD_DIM ** -0.5)
            if prev is None:
                kc = rows(k_ref, cur)
                vc = rows(v_ref, cur)
                hmask = head0
            else:
                kc = jnp.concatenate([rows(k_ref, prev), rows(k_ref, cur)], axis=0)
                vc = jnp.concatenate([rows(v_ref, prev), rows(v_ref, cur)], axis=0)
                hmask = head0_2
                pen = jnp.where(first, NEG_INF, 0.0) * is_prev
            kb = kc.astype(BF16)
            for hh in range(2):
                sel = head0 if hh == 0 else jnp.logical_not(head0)
                vsel = hmask if hh == 0 else jnp.logical_not(hmask)
                qh = jnp.where(sel, qb, 0.0).astype(BF16)
                values.append(jnp.where(vsel, vc, 0.0).astype(BF16))
                sc = lax.dot_general(qh, kb, (((1,), (1,)), ((), ())),
                                     preferred_element_type=F32)
                if prev is None:
                    sc = sc + bias_ref[p, hh, :, blk:2 * blk]
                else:
                    sc = sc + bias_ref[p, hh] + pen
                scores.append(sc)
        probs = []
        stats = []
        for sc in scores:
            m = jnp.max(sc, axis=1, keepdims=True)
            e = jnp.exp(sc - m)
            stats.append((m, jnp.sum(e, axis=1, keepdims=True)))
            probs.append(e.astype(BF16))
        outs = [jnp.dot(e, vh, preferred_element_type=F32) for e, vh in zip(probs, values)]
        for n, (cur, prev, first) in enumerate(specs):
            (m0, l0), (m1, l1) = stats[2 * n], stats[2 * n + 1]
            if dil == 1:
                idx = pl.ds(cur, blk)
            else:
                idx = pl.ds(cur, blk, stride=dil)
            oacc_ref[p, idx, :] = outs[2 * n] + outs[2 * n + 1]
            m_ref[p, idx, :] = jnp.where(head0, jnp.broadcast_to(m0, (blk, LANES)),
                                         jnp.broadcast_to(m1, (blk, LANES)))
            l_ref[p, idx, :] = jnp.where(head0, jnp.broadcast_to(l0, (blk, LANES)),
                                         jnp.broadcast_to(l1, (blk, LANES)))

    for p, (window, dil) in enumerate(DILATED_PATTERNS):
        stream_len = s_len // dil
        nb = stream_len // blk

        if nb == 1:
            def body(r4, c, p=p, dil=dil):
                units(p, dil, [(r4 * ATTN_UNROLL + uu, None, None) for uu in range(ATTN_UNROLL)])
                return c
            lax.fori_loop(0, dil // ATTN_UNROLL, body, 0)
        else:
            def body(u4, c, p=p, dil=dil, nb=nb):
                specs = []
                for uu in range(ATTN_UNROLL):
                    u = u4 * ATTN_UNROLL + uu
                    r = u // nb
                    i = u - r * nb
                    cur = r + i * (blk * dil)
                    prev = r + jnp.maximum(i - 1, 0) * (blk * dil)
                    if dil == 1:
                        cur = pl.multiple_of(cur, blk)
                        prev = pl.multiple_of(prev, blk)
                    specs.append((cur, prev, i == 0))
                units(p, dil, specs)
                return c
            lax.fori_loop(0, dil * nb // ATTN_UNROLL, body, 0)

    def merge(i, c):
        r0 = pl.multiple_of(i * 256, 256)
        ms = [m_ref[p, pl.ds(r0, 256), :] for p in range(n_pat)]
        mx = functools.reduce(jnp.maximum, ms)
        num = None
        den = None
        for p in range(n_pat):
            w = jnp.exp(ms[p] - mx)
            tn = w * oacc_ref[p, pl.ds(r0, 256), :]
            td = w * l_ref[p, pl.ds(r0, 256), :]
            num = tn if num is None else num + tn
            den = td if den is None else den + td
        o_ref[pl.ds(r0, 256), :] = (num / den).astype(o_ref.dtype)
        return c

    lax.fori_loop(0, s_len // 256, merge, 0)


def _attention(qkv, slopes, batch, seq):
    n = qkv.shape[0]
    n_hp = D_ATTN // LANES
    n_pat = len(DILATED_PATTERNS)
    return pl.pallas_call(
        _attn_kernel,
        grid=(batch, n_hp),
        in_specs=[
            pl.BlockSpec(memory_space=pltpu.SMEM),
            pl.BlockSpec((seq, LANES), lambda b, h: (b, h)),
            pl.BlockSpec((seq, LANES), lambda b, h: (b, n_hp + h)),
            pl.BlockSpec((seq, LANES), lambda b, h: (b, 2 * n_hp + h)),
        ],
        out_specs=pl.BlockSpec((seq, LANES), lambda b, h: (b, h)),
        out_shape=jax.ShapeDtypeStruct((n, D_ATTN), BF16),
        scratch_shapes=[
            pltpu.VMEM((n_pat, 2, ATTN_BLOCK, 2 * ATTN_BLOCK), F32),
            pltpu.VMEM((n_pat, seq, LANES), F32),
            pltpu.VMEM((n_pat, seq, LANES), F32),
            pltpu.VMEM((n_pat, seq, LANES), F32),
        ],
        compiler_params=pltpu.CompilerParams(
            dimension_semantics=("arbitrary", "arbitrary"), vmem_limit_bytes=VMEM_LIMIT),
        name="attn",
    )(slopes, qkv, qkv, qkv)


def _out_proj_kernel(alpha, yc_ref, ya_ref, x_ref, wc_ref, wa_ref, b_ref, g_ref, be_ref, o_ref):
    y = jnp.dot(yc_ref[...], wc_ref[...], preferred_element_type=F32)
    y = y + jnp.dot(ya_ref[...], wa_ref[...], preferred_element_type=F32)
    y = y + b_ref[...]
    o_ref[...] = _ln(alpha * x_ref[...] + y, g_ref[...], be_ref[...])


def _out_proj(yc, ya, x2, w_out_bf, b_out, g, b, alpha, tm=256):
    n, d = x2.shape
    row = lambda i: (i, 0)
    fix = lambda i: (0, 0)
    return pl.pallas_call(
        functools.partial(_out_proj_kernel, alpha),
        grid=(n // tm,),
        in_specs=[
            pl.BlockSpec((tm, D_CONV), row),
            pl.BlockSpec((tm, D_ATTN), row),
            pl.BlockSpec((tm, d), row),
            pl.BlockSpec((D_CONV, d), fix),
            pl.BlockSpec((D_ATTN, d), fix),
            pl.BlockSpec((1, d), fix),
            pl.BlockSpec((1, d), fix),
            pl.BlockSpec((1, d), fix),
        ],
        out_specs=pl.BlockSpec((tm, d), row),
        out_shape=jax.ShapeDtypeStruct((n, d), F32),
        compiler_params=pltpu.CompilerParams(
            dimension_semantics=("arbitrary",), vmem_limit_bytes=VMEM_LIMIT),
        name="out_proj",
    )(yc, ya, x2, w_out_bf[:D_CONV], w_out_bf[D_CONV:], b_out.reshape(1, d),
      g.reshape(1, d), b.reshape(1, d))


ROUTE_TOKENS = 256


def _route_kernel(h_ref, wq_ref, keys_ref, idx_ref, g_ref, q_scr, gt_scr, it_scr):
    t = h_ref.shape[0]
    k_top = PEER_TOPK
    nk = PEER_NKEYS
    q_scr[...] = jnp.dot(h_ref[...].astype(BF16), wq_ref[...], preferred_element_type=F32)

    lt = LANES
    iota_n = lax.broadcasted_iota(I32, (nk, lt), 0)
    iota_k = lax.broadcasted_iota(I32, (k_top, lt), 0)
    iota_8 = lax.broadcasted_iota(I32, (SUBLANES, lt), 0)
    neg = -jnp.inf

    half = k_top // 2
    blocks = ([(0, 0), (0, half)] + [(i, 0) for i in range(1, half)])
    pos = jnp.concatenate(
        [iota_8 + (i * k_top + j0) for (i, j0) in blocks] + [(iota_8 + half) * k_top], axis=0)
    n_cand = pos.shape[0]

    def head(hl, c):
        h = hl // (t // lt)
        tile = hl - h * (t // lt)
        lane0 = pl.multiple_of(tile * lt, lt)
        top_s = []
        top_i = []
        for p in range(2):
            col = pl.multiple_of(h * PEER_QDIM + p * (PEER_QDIM // 2), PEER_QDIM // 2)
            q = q_scr[pl.ds(lane0, lt), pl.ds(col, PEER_QDIM // 2)].astype(BF16)
            s = lax.dot_general(keys_ref[h, p], q, (((1,), (1,)), ((), ())),
                                preferred_element_type=F32)
            ts = jnp.zeros((k_top, lt), F32)
            ti = jnp.zeros((k_top, lt), I32)
            for k in range(k_top):
                m = jnp.max(s, axis=0, keepdims=True)
                am = jnp.min(jnp.where(s == m, iota_n, nk), axis=0, keepdims=True)
                s = jnp.where(iota_n == am, neg, s)
                ts = jnp.where(iota_k == k, m, ts)
                ti = jnp.where(iota_k == k, am, ti)
            top_s.append(ts)
            top_i.append(ti)
        sa, sb = top_s
        ia, ib = top_i
        cand = jnp.concatenate(
            [sa[i:i + 1, :] + sb[j0:j0 + SUBLANES, :] for (i, j0) in blocks]
            + [sa[half:, :] + sb[0:1, :]], axis=0)
        eid = jnp.concatenate(
            [ia[i:i + 1, :] * nk + ib[j0:j0 + SUBLANES, :] for (i, j0) in blocks]
            + [ia[half:, :] * nk + ib[0:1, :]], axis=0)
        bs = jnp.zeros((k_top, lt), F32)
        be = jnp.zeros((k_top, lt), I32)
        for k in range(k_top):
            m = jnp.max(cand, axis=0, keepdims=True)
            am = jnp.min(jnp.where(cand == m, pos, k_top * k_top), axis=0, keepdims=True)
            sel = pos == am
            e = jnp.max(jnp.where(sel, eid, -1), axis=0, keepdims=True)
            cand = jnp.where(sel, neg, cand)
            bs = jnp.where(iota_k == k, m, bs)
            be = jnp.where(iota_k == k, e, be)
        ex = jnp.exp(bs - bs[0:1, :])
        gate = ex / jnp.sum(ex, axis=0, keepdims=True)
        r0 = pl.multiple_of(h * k_top, k_top)
        gt_scr[pl.ds(r0, k_top), pl.ds(lane0, lt)] = gate
        it_scr[pl.ds(r0, k_top), pl.ds(lane0, lt)] = be
        return c

    lax.fori_loop(0, PEER_HEADS * (t // lt), head, 0)
    g_ref[...] = gt_scr[...].T
    idx_ref[...] = it_scr[...].T


def _route(h1, wq_bf, keys_bf, tm=ROUTE_TOKENS):
    n, d = h1.shape
    hk = PEER_HEADS * PEER_TOPK
    return pl.pallas_call(
        _route_kernel,
        grid=(n // tm,),
        in_specs=[
            pl.BlockSpec((tm, d), lambda i: (i, 0)),
            pl.BlockSpec(wq_bf.shape, lambda i: (0, 0)),
            pl.BlockSpec(keys_bf.shape, lambda i: (0, 0, 0, 0)),
        ],
        out_specs=[
            pl.BlockSpec((tm, hk), lambda i: (i, 0)),
            pl.BlockSpec((tm, hk), lambda i: (i, 0)),
        ],
        out_shape=[
            jax.ShapeDtypeStruct((n, hk), I32),
            jax.ShapeDtypeStruct((n, hk), F32),
        ],
        scratch_shapes=[
            pltpu.VMEM((tm, PEER_HEADS * PEER_QDIM), F32),
            pltpu.VMEM((hk, tm), F32),
            pltpu.VMEM((hk, tm), I32),
        ],
        compiler_params=pltpu.CompilerParams(
            dimension_semantics=("arbitrary",), vmem_limit_bytes=VMEM_LIMIT),
        name="route",
    )(h1, wq_bf, keys_bf)


def _peer_kernel(alpha, idx0_ref, idxa_ref, h_ref, g_ref, uv_ref, lg_ref, lb_ref, o_ref, *scratch):
    bufs = scratch[:PEER_ALLOCS]
    sem, y_scr, acc_scr, wb_scr = scratch[PEER_ALLOCS:]
    tb, d = h_ref.shape
    hk = g_ref.shape[1]
    n_lt = d // LANES
    n_kg = hk // SUBLANES
    st = PEER_SET
    ahead_sets = PEER_AHEAD // st
    step = pl.program_id(0)

    def issue_group(idx_ref, tok, alloc, slot, kg):
        for r in range(SUBLANES):
            e = idx_ref[tok * hk + kg * SUBLANES + r]
            pltpu.make_async_copy(
                uv_ref.at[e], bufs[alloc].at[slot, kg, :, r, :],
                sem.at[alloc, slot]).start(priority=r % 2)

    def wait(alloc, slot):
        pltpu.make_async_copy(
            bufs[alloc].at[slot], bufs[alloc].at[slot], sem.at[alloc, slot]).wait()

    eye = (lax.broadcasted_iota(I32, (hk, hk), 0) == lax.broadcasted_iota(I32, (hk, hk), 1))
    ones_row = jnp.ones((SUBLANES, LANES), BF16)
    ones_sq = jnp.ones((hk, LANES), BF16)
    sub = lax.broadcasted_iota(I32, (SUBLANES, LANES), 0)

    def gate_chain(tok0):
        acc = acc_scr[...].reshape(st * hk, LANES).astype(BF16)
        sums = lax.dot_general(ones_row, acc, (((1,), (1,)), ((), ())),
                               preferred_element_type=F32)
        a = jnp.zeros((st, hk), F32)
        for j in range(st):
            a = jnp.where(sub == j, sums[:, j * hk:(j + 1) * hk], a)
        act = 0.5 * a * (1.0 + lax.erf(a * (2.0 ** -0.5)))
        w = act * g_ref[pl.ds(tok0, st), :]
        diag = jnp.concatenate(
            [jnp.where(eye, jnp.broadcast_to(w[j:j + 1, :], (hk, hk)), 0.0).astype(BF16)
             for j in range(st)], axis=0)
        wb = jnp.dot(diag, ones_sq, preferred_element_type=F32)
        wb_scr[...] = wb.reshape(st, hk, LANES)

    @pl.when(step == 0)
    def _():
        for a in range(ahead_sets):
            def first(j, c, a=a):
                for kg in range(n_kg):
                    issue_group(idx0_ref, a * st + j, a, j, kg)
                return c
            lax.fori_loop(0, st, first, 0)

    body_tokens = PEER_ALLOCS * st

    def body(it, carry):
        base = pl.multiple_of(it * body_tokens, body_tokens)
        for q in range(PEER_ALLOCS):
            buf = bufs[q]
            nxt = (q + ahead_sets) % PEER_ALLOCS
            tok0 = base + q * st
            for j in range(st):
                wait(q, j)

            def loop_a(i, c, buf=buf, nxt=nxt, tok0=tok0):
                for j in range(st):
                    issue_group(idxa_ref, tok0 + j, nxt, j, i)
                    x_row = h_ref[pl.ds(tok0 + j, 1), :]
                    for g2 in range(2):
                        kg = 2 * i + g2
                        acc = None
                        for c2 in range(n_lt):
                            xc = jnp.broadcast_to(x_row[:, c2 * LANES:(c2 + 1) * LANES],
                                                  (SUBLANES, LANES))
                            term = buf[j, kg, c2, :, :] * xc
                            acc = term if acc is None else acc + term
                        r0 = pl.multiple_of(kg * SUBLANES, SUBLANES)
                        acc_scr[j, pl.ds(r0, SUBLANES), :] = acc
                return c

            lax.fori_loop(0, n_kg // 2, loop_a, 0, unroll=True)

            gate_chain(pl.multiple_of(tok0, st))

            def loop_b(c2, c, buf=buf, nxt=nxt, tok0=tok0):
                for j in range(st):
                    issue_group(idxa_ref, tok0 + j, nxt, j, n_kg // 2 + c2)
                tile = jnp.zeros((SUBLANES, LANES), F32)
                for j in range(st):
                    wb = wb_scr[j].reshape(n_kg, SUBLANES, LANES)
                    prod = buf[j, :, n_lt + c2, :, :] * wb
                    row = jnp.sum(jnp.sum(prod, axis=0), axis=0, keepdims=True)
                    tile = jnp.where(sub == j, jnp.broadcast_to(row, (SUBLANES, LANES)), tile)
                col = pl.multiple_of(c2 * LANES, LANES)
                y_scr[pl.ds(pl.multiple_of(tok0, st), st), pl.ds(col, LANES)] = tile
                return c

            lax.fori_loop(0, n_lt, loop_b, 0, unroll=True)
        return carry

    lax.fori_loop(0, tb // body_tokens, body, 0)

    @pl.when(step == pl.num_programs(0) - 1)
    def _():
        for t in range(PEER_AHEAD):
            wait((t // st) % PEER_ALLOCS, t % st)

    o_ref[...] = _ln(alpha * h_ref[...] + y_scr[...], lg_ref[...], lb_ref[...])


def _peer(idx, h1, gates, uv3, g, b, alpha, tb=PEER_TOKENS):
    n, d = h1.shape
    hk = gates.shape[1]
    assert SUBLANES == PEER_SET and PEER_AHEAD % PEER_SET == 0
    assert 0 < PEER_AHEAD // PEER_SET < PEER_ALLOCS - 1
    assert hk // SUBLANES == 2 * (d // LANES)
    assert tb % (PEER_ALLOCS * PEER_SET) == 0 and n % tb == 0
    idx_ahead = jnp.concatenate(
        [idx[PEER_AHEAD:], jnp.broadcast_to(idx[-1:], (PEER_AHEAD, hk))], axis=0).reshape(-1)
    idx_first = idx[:PEER_AHEAD].reshape(-1)
    row = lambda i: (i, 0)
    fix = lambda i: (0, 0)
    slot_shape = (PEER_SET, hk // SUBLANES, uv3.shape[1], SUBLANES, LANES)
    return pl.pallas_call(
        functools.partial(_peer_kernel, alpha),
        grid=(n // tb,),
        in_specs=[
            pl.BlockSpec(memory_space=pltpu.SMEM),
            pl.BlockSpec((tb * hk,), lambda i: (i,), memory_space=pltpu.SMEM),
            pl.BlockSpec((tb, d), row),
            pl.BlockSpec((tb, hk), row),
            pl.BlockSpec(memory_space=pl.ANY),
            pl.BlockSpec((1, d), fix),
            pl.BlockSpec((1, d), fix),
        ],
        out_specs=pl.BlockSpec((tb, d), row),
        out_shape=jax.ShapeDtypeStruct((n, d), F32),
        scratch_shapes=(
            [pltpu.VMEM(slot_shape, F32) for _ in range(PEER_ALLOCS)]
            + [pltpu.SemaphoreType.DMA((PEER_ALLOCS, PEER_SET)),
               pltpu.VMEM((tb, d), F32),
               pltpu.VMEM((PEER_SET, hk, LANES), F32),
               pltpu.VMEM((PEER_SET, hk, LANES), F32)]),
        compiler_params=pltpu.CompilerParams(
            dimension_semantics=("arbitrary",), vmem_limit_bytes=VMEM_LIMIT),
        name="peer",
    )(idx_first, idx_ahead, h1, gates, uv3, g.reshape(1, d), b.reshape(1, d))


def kernel(x, w_in, b_in, conv_w, conv_b, conv_ln_g, conv_ln_b, w_out, b_out, ln1_g, ln1_b,
           peer_wq, peer_keys, peer_u, peer_v, ln2_g, ln2_b):
    batch, seq, d = x.shape
    depth = w_in.shape[0]
    alpha = (2.0 * depth) ** 0.25
    slopes = 2.0 ** (-8.0 * jnp.arange(1, N_HEADS + 1, dtype=F32) / N_HEADS)
    h = x.reshape(batch * seq, d)
    for l in range(depth):
        u, qkv = _in_proj(h, w_in[l].astype(BF16), b_in[l])
        yc = _conv(u, conv_w[l], conv_b[l], conv_ln_g[l], conv_ln_b[l], batch, seq)
        ya = _attention(qkv, slopes, batch, seq)
        h1 = _out_proj(yc, ya, h, w_out[l].astype(BF16), b_out[l], ln1_g[l], ln1_b[l], alpha)
        idx, gates = _route(h1, peer_wq[l].astype(BF16), peer_keys[l].astype(BF16))
        uv3 = jnp.concatenate([peer_u[l], peer_v[l]], axis=1).reshape(-1, 2 * d // LANES, LANES)
        h = _peer(idx, h1, gates, uv3, ln2_g[l], ln2_b[l], alpha)
    return h.reshape(batch, seq, d)
```

```python
import functools
import math

import jax
import jax.numpy as jnp
from jax import lax
from jax.experimental import pallas as pl
from jax.experimental.pallas import tpu as pltpu

F32 = jnp.float32
BF16 = jnp.bfloat16
I32 = jnp.int32

D_MODEL = 1024
D_CONV = D_MODEL // 2
D_ATTN = D_MODEL - D_CONV
HEAD_DIM = 64
N_HEADS = D_ATTN // HEAD_DIM
CONV_WIDTH = 31
DILATED_PATTERNS = ((128, 1), (512, 4), (2048, 16))
ATTN_BLOCK = 128
ATTN_UNROLL = 4
PEER_HEADS = 8
PEER_NKEYS = 128
PEER_QDIM = 256
PEER_TOPK = 16
LN_EPS = 1e-5
NEG_INF = -1e30

LANES = 128
SUBLANES = 8
VMEM_LIMIT = 56 * 1024 * 1024

PEER_SET = 8
PEER_ALLOCS = 4
PEER_AHEAD = 2 * PEER_SET
PEER_TOKENS = 128


def _ln(x, g, b):
    mu = jnp.mean(x, axis=-1, keepdims=True)
    xc = x - mu
    var = jnp.mean(xc * xc, axis=-1, keepdims=True)
    return xc * lax.rsqrt(var + LN_EPS) * g + b


def _in_proj_kernel(x_ref, w_ref, b_ref, u_ref, qkv_ref):
    x = x_ref[...].astype(BF16)
    nu = u_ref.shape[1]
    u_ref[...] = jnp.dot(x, w_ref[:, :nu], preferred_element_type=F32) + b_ref[:, :nu]
    qkv_ref[...] = jnp.dot(x, w_ref[:, nu:], preferred_element_type=F32) + b_ref[:, nu:]


def _in_proj(x2, w_bf, b, tm=256):
    n, d = x2.shape
    dout = w_bf.shape[1]
    nu = 2 * D_CONV
    return pl.pallas_call(
        _in_proj_kernel,
        grid=(n // tm,),
        in_specs=[
            pl.BlockSpec((tm, d), lambda i: (i, 0)),
            pl.BlockSpec((d, dout), lambda i: (0, 0)),
            pl.BlockSpec((1, dout), lambda i: (0, 0)),
        ],
        out_specs=[
            pl.BlockSpec((tm, nu), lambda i: (i, 0)),
            pl.BlockSpec((tm, dout - nu), lambda i: (i, 0)),
        ],
        out_shape=[
            jax.ShapeDtypeStruct((n, nu), F32),
            jax.ShapeDtypeStruct((n, dout - nu), F32),
        ],
        compiler_params=pltpu.CompilerParams(
            dimension_semantics=("arbitrary",), vmem_limit_bytes=VMEM_LIMIT),
        name="in_proj",
    )(x2, w_bf, b.reshape(1, dout))


CONV_ROWS = 64
CONV_PAD = 32


def _conv_kernel(u_ref, w_ref, cb_ref, g_ref, b_ref, o_ref, pad_ref):
    s = o_ref.shape[0]
    pad_ref[0:CONV_PAD, :] = jnp.zeros((CONV_PAD, D_CONV), F32)

    def glu(i, c):
        r0 = pl.multiple_of(i * 256, 256)
        a = u_ref[pl.ds(r0, 256), 0:D_CONV]
        gate = u_ref[pl.ds(r0, 256), D_CONV:2 * D_CONV]
        pad_ref[pl.ds(CONV_PAD + r0, 256), :] = a * jax.nn.sigmoid(gate)
        return c

    lax.fori_loop(0, s // 256, glu, 0)

    off = CONV_PAD - (CONV_WIDTH - 1)

    def tile(i, c):
        r0 = pl.multiple_of(i * CONV_ROWS, CONV_ROWS)
        acc = jnp.broadcast_to(cb_ref[...], (CONV_ROWS, D_CONV))
        win = pad_ref[pl.ds(r0, CONV_ROWS + CONV_PAD), :]
        for j in range(CONV_WIDTH):
            acc = acc + w_ref[j:j + 1, :] * win[off + j:off + j + CONV_ROWS, :]
        y = _ln(acc, g_ref[...], b_ref[...])
        o_ref[pl.ds(r0, CONV_ROWS), :] = (y * jax.nn.sigmoid(y)).astype(o_ref.dtype)
        return c

    lax.fori_loop(0, s // CONV_ROWS, tile, 0)


def _conv(u, conv_w, conv_b, g, b, batch, seq):
    n = u.shape[0]
    return pl.pallas_call(
        _conv_kernel,
        grid=(batch,),
        in_specs=[
            pl.BlockSpec((seq, 2 * D_CONV), lambda i: (i, 0)),
            pl.BlockSpec((CONV_WIDTH, D_CONV), lambda i: (0, 0)),
            pl.BlockSpec((1, D_CONV), lambda i: (0, 0)),
            pl.BlockSpec((1, D_CONV), lambda i: (0, 0)),
            pl.BlockSpec((1, D_CONV), lambda i: (0, 0)),
        ],
        out_specs=pl.BlockSpec((seq, D_CONV), lambda i: (i, 0)),
        out_shape=jax.ShapeDtypeStruct((n, D_CONV), BF16),
        scratch_shapes=[pltpu.VMEM((seq + CONV_PAD, D_CONV), F32)],
        compiler_params=pltpu.CompilerParams(
            dimension_semantics=("arbitrary",), vmem_limit_bytes=VMEM_LIMIT),
        name="conv",
    )(u, conv_w, conv_b.reshape(1, -1), g.reshape(1, -1), b.reshape(1, -1))


def _attn_kernel(slopes_ref, q_ref, k_ref, v_ref, o_ref, bias_ref, oacc_ref, m_ref, l_ref):
    blk = ATTN_BLOCK
    s_len = q_ref.shape[0]
    hp = pl.program_id(1)
    n_pat = len(DILATED_PATTERNS)

    lane = lax.broadcasted_iota(I32, (blk, LANES), 1)
    head0 = lane < HEAD_DIM
    lane2 = lax.broadcasted_iota(I32, (2 * blk, LANES), 1)
    head0_2 = lane2 < HEAD_DIM

    qi = lax.broadcasted_iota(I32, (blk, 2 * blk), 0)
    kj = lax.broadcasted_iota(I32, (blk, 2 * blk), 1)
    dist = qi + blk - kj
    is_prev = (kj < blk).astype(F32)
    for p, (window, dil) in enumerate(DILATED_PATTERNS):
        n_back = window // dil
        valid = (dist >= 0) & (dist <= n_back)
        dpos = (dist * dil).astype(F32)
        for hh in range(2):
            slope = slopes_ref[2 * hp + hh]
            bias_ref[p, hh] = jnp.where(valid, -slope * dpos, NEG_INF)

    def units(p, dil, specs):
        def rows(ref, start):
            if dil == 1:
                return ref[pl.ds(start, blk), :]
            return ref[pl.ds(start, blk, stride=dil), :]

        scores = []
        values = []
        for cur, prev, first in specs:
            qb = rows(q_ref, cur) * (HEAD_DIM ** -0.5)
            if prev is None:
                kc = rows(k_ref, cur)
                vc = rows(v_ref, cur)
                hmask = head0
            else:
                kc = jnp.concatenate([rows(k_ref, prev), rows(k_ref, cur)], axis=0)
                vc = jnp.concatenate([rows(v_ref, prev), rows(v_ref, cur)], axis=0)
                hmask = head0_2
                pen = jnp.where(first, NEG_INF, 0.0) * is_prev
            kb = kc.astype(BF16)
            for hh in range(2):
                sel = head0 if hh == 0 else jnp.logical_not(head0)
                vsel = hmask if hh == 0 else jnp.logical_not(hmask)
                qh = jnp.where(sel, qb, 0.0).astype(BF16)
                values.append(jnp.where(vsel, vc, 0.0).astype(BF16))
                sc = lax.dot_general(qh, kb, (((1,), (1,)), ((), ())),
                                     preferred_element_type=F32)
                if prev is None:
                    sc = sc + bias_ref[p, hh, :, blk:2 * blk]
                else:
                    sc = sc + bias_ref[p, hh] + pen
                scores.append(sc)
        probs = []
        stats = []
        for sc in scores:
            m = jnp.max(sc, axis=1, keepdims=True)
            e = jnp.exp(sc - m)
            stats.append((m, jnp.sum(e, axis=1, keepdims=True)))
            probs.append(e.astype(BF16))
        outs = [jnp.dot(e, vh, preferred_element_type=F32) for e, vh in zip(probs, values)]
        for n, (cur, prev, first) in enumerate(specs):
            (m0, l0), (m1, l1) = stats[2 * n], stats[2 * n + 1]
            if dil == 1:
                idx = pl.ds(cur, blk)
            else:
                idx = pl.ds(cur, blk, stride=dil)
            oacc_ref[p, idx, :] = outs[2 * n] + outs[2 * n + 1]
            m_ref[p, idx, :] = jnp.where(head0, jnp.broadcast_to(m0, (blk, LANES)),
                                         jnp.broadcast_to(m1, (blk, LANES)))
            l_ref[p, idx, :] = jnp.where(head0, jnp.broadcast_to(l0, (blk, LANES)),
                                         jnp.broadcast_to(l1, (blk, LANES)))

    for p, (window, dil) in enumerate(DILATED_PATTERNS):
        stream_len = s_len // dil
        nb = stream_len // blk

        if nb == 1:
            def body(r4, c, p=p, dil=dil):
                units(p, dil, [(r4 * ATTN_UNROLL + uu, None, None) for uu in range(ATTN_UNROLL)])
                return c
            lax.fori_loop(0, dil // ATTN_UNROLL, body, 0)
        else:
            def body(u4, c, p=p, dil=dil, nb=nb):
                specs = []
                for uu in range(ATTN_UNROLL):
                    u = u4 * ATTN_UNROLL + uu
                    r = u // nb
                    i = u - r * nb
                    cur = r + i * (blk * dil)
                    prev = r + jnp.maximum(i - 1, 0) * (blk * dil)
                    if dil == 1:
                        cur = pl.multiple_of(cur, blk)
                        prev = pl.multiple_of(prev, blk)
                    specs.append((cur, prev, i == 0))
                units(p, dil, specs)
                return c
            lax.fori_loop(0, dil * nb // ATTN_UNROLL, body, 0)

    def merge(i, c):
        r0 = pl.multiple_of(i * 256, 256)
        ms = [m_ref[p, pl.ds(r0, 256), :] for p in range(n_pat)]
        mx = functools.reduce(jnp.maximum, ms)
        num = None
        den = None
        for p in range(n_pat):
            w = jnp.exp(ms[p] - mx)
            tn = w * oacc_ref[p, pl.ds(r0, 256), :]
            td = w * l_ref[p, pl.ds(r0, 256), :]
            num = tn if num is None else num + tn
            den = td if den is None else den + td
        o_ref[pl.ds(r0, 256), :] = (num / den).astype(o_ref.dtype)
        return c

    lax.fori_loop(0, s_len // 256, merge, 0)


def _attention(qkv, slopes, batch, seq):
    n = qkv.shape[0]
    n_hp = D_ATTN // LANES
    n_pat = len(DILATED_PATTERNS)
    return pl.pallas_call(
        _attn_kernel,
        grid=(batch, n_hp),
        in_specs=[
            pl.BlockSpec(memory_space=pltpu.SMEM),
            pl.BlockSpec((seq, LANES), lambda b, h: (b, h)),
            pl.BlockSpec((seq, LANES), lambda b, h: (b, n_hp + h)),
            pl.BlockSpec((seq, LANES), lambda b, h: (b, 2 * n_hp + h)),
        ],
        out_specs=pl.BlockSpec((seq, LANES), lambda b, h: (b, h)),
        out_shape=jax.ShapeDtypeStruct((n, D_ATTN), BF16),
        scratch_shapes=[
            pltpu.VMEM((n_pat, 2, ATTN_BLOCK, 2 * ATTN_BLOCK), F32),
            pltpu.VMEM((n_pat, seq, LANES), F32),
            pltpu.VMEM((n_pat, seq, LANES), F32),
            pltpu.VMEM((n_pat, seq, LANES), F32),
        ],
        compiler_params=pltpu.CompilerParams(
            dimension_semantics=("arbitrary", "arbitrary"), vmem_limit_bytes=VMEM_LIMIT),
        name="attn",
    )(slopes, qkv, qkv, qkv)


def _out_proj_kernel(alpha, yc_ref, ya_ref, x_ref, wc_ref, wa_ref, b_ref, g_ref, be_ref, o_ref):
    y = jnp.dot(yc_ref[...], wc_ref[...], preferred_element_type=F32)
    y = y + jnp.dot(ya_ref[...], wa_ref[...], preferred_element_type=F32)
    y = y + b_ref[...]
    o_ref[...] = _ln(alpha * x_ref[...] + y, g_ref[...], be_ref[...])


def _out_proj(yc, ya, x2, w_out_bf, b_out, g, b, alpha, tm=256):
    n, d = x2.shape
    row = lambda i: (i, 0)
    fix = lambda i: (0, 0)
    return pl.pallas_call(
        functools.partial(_out_proj_kernel, alpha),
        grid=(n // tm,),
        in_specs=[
            pl.BlockSpec((tm, D_CONV), row),
            pl.BlockSpec((tm, D_ATTN), row),
            pl.BlockSpec((tm, d), row),
            pl.BlockSpec((D_CONV, d), fix),
            pl.BlockSpec((D_ATTN, d), fix),
            pl.BlockSpec((1, d), fix),
            pl.BlockSpec((1, d), fix),
            pl.BlockSpec((1, d), fix),
        ],
        out_specs=pl.BlockSpec((tm, d), row),
        out_shape=jax.ShapeDtypeStruct((n, d), F32),
        compiler_params=pltpu.CompilerParams(
            dimension_semantics=("arbitrary",), vmem_limit_bytes=VMEM_LIMIT),
        name="out_proj",
    )(yc, ya, x2, w_out_bf[:D_CONV], w_out_bf[D_CONV:], b_out.reshape(1, d),
      g.reshape(1, d), b.reshape(1, d))


ROUTE_TOKENS = 256


def _route_kernel(h_ref, wq_ref, keys_ref, idx_ref, g_ref, q_scr, gt_scr, it_scr):
    t = h_ref.shape[0]
    k_top = PEER_TOPK
    nk = PEER_NKEYS
    q_scr[...] = jnp.dot(h_ref[...].astype(BF16), wq_ref[...], preferred_element_type=F32)

    lt = LANES
    iota_n = lax.broadcasted_iota(I32, (nk, lt), 0)
    iota_k = lax.broadcasted_iota(I32, (k_top, lt), 0)
    iota_8 = lax.broadcasted_iota(I32, (SUBLANES, lt), 0)
    neg = -jnp.inf

    half = k_top // 2
    blocks = ([(0, 0), (0, half)] + [(i, 0) for i in range(1, half)])
    pos = jnp.concatenate(
        [iota_8 + (i * k_top + j0) for (i, j0) in blocks] + [(iota_8 + half) * k_top], axis=0)
    n_cand = pos.shape[0]

    def head(hl, c):
        h = hl // (t // lt)
        tile = hl - h * (t // lt)
        lane0 = pl.multiple_of(tile * lt, lt)
        top_s = []
        top_i = []
        for p in range(2):
            col = pl.multiple_of(h * PEER_QDIM + p * (PEER_QDIM // 2), PEER_QDIM // 2)
            q = q_scr[pl.ds(lane0, lt), pl.ds(col, PEER_QDIM // 2)].astype(BF16)
            s = lax.dot_general(keys_ref[h, p], q, (((1,), (1,)), ((), ())),
                                preferred_element_type=F32)
            ts = jnp.zeros((k_top, lt), F32)
            ti = jnp.zeros((k_top, lt), I32)
            for k in range(k_top):
                m = jnp.max(s, axis=0, keepdims=True)
                am = jnp.min(jnp.where(s == m, iota_n, nk), axis=0, keepdims=True)
                s = jnp.where(iota_n == am, neg, s)
                ts = jnp.where(iota_k == k, m, ts)
                ti = jnp.where(iota_k == k, am, ti)
            top_s.append(ts)
            top_i.append(ti)
        sa, sb = top_s
        ia, ib = top_i
        cand = jnp.concatenate(
            [sa[i:i + 1, :] + sb[j0:j0 + SUBLANES, :] for (i, j0) in blocks]
            + [sa[half:, :] + sb[0:1, :]], axis=0)
        eid = jnp.concatenate(
            [ia[i:i + 1, :] * nk + ib[j0:j0 + SUBLANES, :] for (i, j0) in blocks]
            + [ia[half:, :] * nk + ib[0:1, :]], axis=0)
        bs = jnp.zeros((k_top, lt), F32)
        be = jnp.zeros((k_top, lt), I32)
        for k in range(k_top):
            m = jnp.max(cand, axis=0, keepdims=True)
            am = jnp.min(jnp.where(cand == m, pos, k_top * k_top), axis=0, keepdims=True)
            sel = pos == am
            e = jnp.max(jnp.where(sel, eid, -1), axis=0, keepdims=True)
            cand = jnp.where(sel, neg, cand)
            bs = jnp.where(iota_k == k, m, bs)
            be = jnp.where(iota_k == k, e, be)
        ex = jnp.exp(bs - bs[0:1, :])
        gate = ex / jnp.sum(ex, axis=0, keepdims=True)
        r0 = pl.multiple_of(h * k_top, k_top)
        gt_scr[pl.ds(r0, k_top), pl.ds(lane0, lt)] = gate
        it_scr[pl.ds(r0, k_top), pl.ds(lane0, lt)] = be
        return c

    lax.fori_loop(0, PEER_HEADS * (t // lt), head, 0)
    g_ref[...] = gt_scr[...].T
    idx_ref[...] = it_scr[...].T


def _route(h1, wq_bf, keys_bf, tm=ROUTE_TOKENS):
    n, d = h1.shape
    hk = PEER_HEADS * PEER_TOPK
    return pl.pallas_call(
        _route_kernel,
        grid=(n // tm,),
        in_specs=[
            pl.BlockSpec((tm, d), lambda i: (i, 0)),
            pl.BlockSpec(wq_bf.shape, lambda i: (0, 0)),
            pl.BlockSpec(keys_bf.shape, lambda i: (0, 0, 0, 0)),
        ],
        out_specs=[
            pl.BlockSpec((tm, hk), lambda i: (i, 0)),
            pl.BlockSpec((tm, hk), lambda i: (i, 0)),
        ],
        out_shape=[
            jax.ShapeDtypeStruct((n, hk), I32),
            jax.ShapeDtypeStruct((n, hk), F32),
        ],
        scratch_shapes=[
            pltpu.VMEM((tm, PEER_HEADS * PEER_QDIM), F32),
            pltpu.VMEM((hk, tm), F32),
            pltpu.VMEM((hk, tm), I32),
        ],
        compiler_params=pltpu.CompilerParams(
            dimension_semantics=("arbitrary",), vmem_limit_bytes=VMEM_LIMIT),
        name="route",
    )(h1, wq_bf, keys_bf)


U32 = jnp.uint32


def _pack_uv(u, v):
    e, d = u.shape
    lo = lax.bitcast_convert_type(u.astype(BF16), jnp.uint16).astype(U32)
    hi = lax.bitcast_convert_type(v.astype(BF16), jnp.uint16).astype(U32)
    return ((hi << 16) | lo).reshape(e, d // LANES, LANES)


def _low_bf16(w):
    return lax.bitcast_convert_type(w << 16, F32)


def _high_bf16(w):
    return lax.bitcast_convert_type(w & jnp.uint32(0xFFFF0000), F32)


def _peer_kernel(alpha, idx0_ref, idxa_ref, h_ref, g_ref, uv_ref, lg_ref, lb_ref, o_ref, *scratch):
    bufs = scratch[:PEER_ALLOCS]
    sem, y_scr, acc_scr, wb_scr = scratch[PEER_ALLOCS:]
    tb, d = h_ref.shape
    hk = g_ref.shape[1]
    n_lt = d // LANES
    n_kg = hk // SUBLANES
    st = PEER_SET
    ahead_sets = PEER_AHEAD // st
    step = pl.program_id(0)

    def issue_group(idx_ref, tok, alloc, slot, kg):
        for r in range(SUBLANES):
            e = idx_ref[tok * hk + kg * SUBLANES + r]
            pltpu.make_async_copy(
                uv_ref.at[e], bufs[alloc].at[slot, kg, :, r, :],
                sem.at[alloc, slot]).start(priority=r % 2)

    def wait(alloc, slot):
        pltpu.make_async_copy(
            bufs[alloc].at[slot], bufs[alloc].at[slot], sem.at[alloc, slot]).wait()

    eye = (lax.broadcasted_iota(I32, (hk, hk), 0) == lax.broadcasted_iota(I32, (hk, hk), 1))
    ones_row = jnp.ones((SUBLANES, LANES), BF16)
    ones_sq = jnp.ones((hk, LANES), BF16)
    sub = lax.broadcasted_iota(I32, (SUBLANES, LANES), 0)

    def gate_chain(tok0):
        acc = acc_scr[...].reshape(st * hk, LANES).astype(BF16)
        sums = lax.dot_general(ones_row, acc, (((1,), (1,)), ((), ())),
                               preferred_element_type=F32)
        a = jnp.zeros((st, hk), F32)
        for j in range(st):
            a = jnp.where(sub == j, sums[:, j * hk:(j + 1) * hk], a)
        act = 0.5 * a * (1.0 + lax.erf(a * (2.0 ** -0.5)))
        w = act * g_ref[pl.ds(tok0, st), :]
        diag = jnp.concatenate(
            [jnp.where(eye, jnp.broadcast_to(w[j:j + 1, :], (hk, hk)), 0.0).astype(BF16)
             for j in range(st)], axis=0)
        wb = jnp.dot(diag, ones_sq, preferred_element_type=F32)
        wb_scr[...] = wb.reshape(st, hk, LANES)

    @pl.when(step == 0)
    def _():
        for a in range(ahead_sets):
            def first(j, c, a=a):
                for kg in range(n_kg):
                    issue_group(idx0_ref, a * st + j, a, j, kg)
                return c
            lax.fori_loop(0, st, first, 0)

    body_tokens = PEER_ALLOCS * st

    def body(it, carry):
        base = pl.multiple_of(it * body_tokens, body_tokens)
        for q in range(PEER_ALLOCS):
            buf = bufs[q]
            nxt = (q + ahead_sets) % PEER_ALLOCS
            tok0 = base + q * st
            for j in range(st):
                wait(q, j)

            def loop_a(i, c, buf=buf, nxt=nxt, tok0=tok0):
                for j in range(st):
                    issue_group(idxa_ref, tok0 + j, nxt, j, i)
                    x_row = h_ref[pl.ds(tok0 + j, 1), :]
                    for g2 in range(2):
                        kg = 2 * i + g2
                        acc = None
                        for c2 in range(n_lt):
                            xc = jnp.broadcast_to(x_row[:, c2 * LANES:(c2 + 1) * LANES],
                                                  (SUBLANES, LANES))
                            term = _low_bf16(buf[j, kg, c2, :, :]) * xc
                            acc = term if acc is None else acc + term
                        r0 = pl.multiple_of(kg * SUBLANES, SUBLANES)
                        acc_scr[j, pl.ds(r0, SUBLANES), :] = acc
                return c

            lax.fori_loop(0, n_kg // 2, loop_a, 0, unroll=True)

            gate_chain(pl.multiple_of(tok0, st))

            def loop_b(c2, c, buf=buf, nxt=nxt, tok0=tok0):
                for j in range(st):
                    issue_group(idxa_ref, tok0 + j, nxt, j, n_kg // 2 + c2)
                tile = jnp.zeros((SUBLANES, LANES), F32)
                for j in range(st):
                    wb = wb_scr[j].reshape(n_kg, SUBLANES, LANES)
                    prod = _high_bf16(buf[j, :, c2, :, :]) * wb
                    row = jnp.sum(jnp.sum(prod, axis=0), axis=0, keepdims=True)
                    tile = jnp.where(sub == j, jnp.broadcast_to(row, (SUBLANES, LANES)), tile)
                col = pl.multiple_of(c2 * LANES, LANES)
                y_scr[pl.ds(pl.multiple_of(tok0, st), st), pl.ds(col, LANES)] = tile
                return c

            lax.fori_loop(0, n_lt, loop_b, 0, unroll=True)
        return carry

    lax.fori_loop(0, tb // body_tokens, body, 0)

    @pl.when(step == pl.num_programs(0) - 1)
    def _():
        for t in range(PEER_AHEAD):
            wait((t // st) % PEER_ALLOCS, t % st)

    o_ref[...] = _ln(alpha * h_ref[...] + y_scr[...], lg_ref[...], lb_ref[...])


def _peer(idx, h1, gates, uv3, g, b, alpha, tb=PEER_TOKENS):
    n, d = h1.shape
    hk = gates.shape[1]
    assert SUBLANES == PEER_SET and PEER_AHEAD % PEER_SET == 0
    assert 0 < PEER_AHEAD // PEER_SET < PEER_ALLOCS - 1
    assert hk // SUBLANES == 2 * (d // LANES) and uv3.shape[1] == d // LANES
    assert tb % (PEER_ALLOCS * PEER_SET) == 0 and n % tb == 0
    idx_ahead = jnp.concatenate(
        [idx[PEER_AHEAD:], jnp.broadcast_to(idx[-1:], (PEER_AHEAD, hk))], axis=0).reshape(-1)
    idx_first = idx[:PEER_AHEAD].reshape(-1)
    row = lambda i: (i, 0)
    fix = lambda i: (0, 0)
    slot_shape = (PEER_SET, hk // SUBLANES, uv3.shape[1], SUBLANES, LANES)
    return pl.pallas_call(
        functools.partial(_peer_kernel, alpha),
        grid=(n // tb,),
        in_specs=[
            pl.BlockSpec(memory_space=pltpu.SMEM),
            pl.BlockSpec((tb * hk,), lambda i: (i,), memory_space=pltpu.SMEM),
            pl.BlockSpec((tb, d), row),
            pl.BlockSpec((tb, hk), row),
            pl.BlockSpec(memory_space=pl.ANY),
            pl.BlockSpec((1, d), fix),
            pl.BlockSpec((1, d), fix),
        ],
        out_specs=pl.BlockSpec((tb, d), row),
        out_shape=jax.ShapeDtypeStruct((n, d), F32),
        scratch_shapes=(
            [pltpu.VMEM(slot_shape, U32) for _ in range(PEER_ALLOCS)]
            + [pltpu.SemaphoreType.DMA((PEER_ALLOCS, PEER_SET)),
               pltpu.VMEM((tb, d), F32),
               pltpu.VMEM((PEER_SET, hk, LANES), F32),
               pltpu.VMEM((PEER_SET, hk, LANES), F32)]),
        compiler_params=pltpu.CompilerParams(
            dimension_semantics=("arbitrary",), vmem_limit_bytes=VMEM_LIMIT),
        name="peer",
    )(idx_first, idx_ahead, h1, gates, uv3, g.reshape(1, d), b.reshape(1, d))


def kernel(x, w_in, b_in, conv_w, conv_b, conv_ln_g, conv_ln_b, w_out, b_out, ln1_g, ln1_b,
           peer_wq, peer_keys, peer_u, peer_v, ln2_g, ln2_b):
    batch, seq, d = x.shape
    depth = w_in.shape[0]
    alpha = (2.0 * depth) ** 0.25
    slopes = 2.0 ** (-8.0 * jnp.arange(1, N_HEADS + 1, dtype=F32) / N_HEADS)
    h = x.reshape(batch * seq, d)
    for l in range(depth):
        u, qkv = _in_proj(h, w_in[l].astype(BF16), b_in[l])
        yc = _conv(u, conv_w[l], conv_b[l], conv_ln_g[l], conv_ln_b[l], batch, seq)
        ya = _attention(qkv, slopes, batch, seq)
        h1 = _out_proj(yc, ya, h, w_out[l].astype(BF16), b_out[l], ln1_g[l], ln1_b[l], alpha)
        idx, gates = _route(h1, peer_wq[l].astype(BF16), peer_keys[l].astype(BF16))
        uv3 = _pack_uv(peer_u[l], peer_v[l])
        h = _peer(idx, h1, gates, uv3, ln2_g[l], ln2_b[l], alpha)
    return h.reshape(batch, seq, d)
```

```python
import functools
import math

import jax
import jax.numpy as jnp
from jax import lax
from jax.experimental import pallas as pl
from jax.experimental.pallas import tpu as pltpu

F32 = jnp.float32
BF16 = jnp.bfloat16
I32 = jnp.int32

D_MODEL = 1024
D_CONV = D_MODEL // 2
D_ATTN = D_MODEL - D_CONV
HEAD_DIM = 64
N_HEADS = D_ATTN // HEAD_DIM
CONV_WIDTH = 31
DILATED_PATTERNS = ((128, 1), (512, 4), (2048, 16))
ATTN_BLOCK = 128
ATTN_UNROLL = 4
PEER_HEADS = 8
PEER_NKEYS = 128
PEER_QDIM = 256
PEER_TOPK = 16
LN_EPS = 1e-5
NEG_INF = -1e30

LANES = 128
SUBLANES = 8
VMEM_LIMIT = 56 * 1024 * 1024

PEER_SET = 8
PEER_ALLOCS = 4
PEER_AHEAD = 2 * PEER_SET
PEER_TOKENS = 128
PEER_RING = 3


def _ln(x, g, b):
    mu = jnp.mean(x, axis=-1, keepdims=True)
    xc = x - mu
    var = jnp.mean(xc * xc, axis=-1, keepdims=True)
    return xc * lax.rsqrt(var + LN_EPS) * g + b


def _in_proj_kernel(x_ref, w_ref, b_ref, u_ref, qkv_ref):
    x = x_ref[...].astype(BF16)
    nu = u_ref.shape[1]
    u_ref[...] = jnp.dot(x, w_ref[:, :nu], preferred_element_type=F32) + b_ref[:, :nu]
    qkv_ref[...] = jnp.dot(x, w_ref[:, nu:], preferred_element_type=F32) + b_ref[:, nu:]


def _in_proj(x2, w_bf, b, tm=256):
    n, d = x2.shape
    dout = w_bf.shape[1]
    nu = 2 * D_CONV
    return pl.pallas_call(
        _in_proj_kernel,
        grid=(n // tm,),
        in_specs=[
            pl.BlockSpec((tm, d), lambda i: (i, 0)),
            pl.BlockSpec((d, dout), lambda i: (0, 0)),
            pl.BlockSpec((1, dout), lambda i: (0, 0)),
        ],
        out_specs=[
            pl.BlockSpec((tm, nu), lambda i: (i, 0)),
            pl.BlockSpec((tm, dout - nu), lambda i: (i, 0)),
        ],
        out_shape=[
            jax.ShapeDtypeStruct((n, nu), F32),
            jax.ShapeDtypeStruct((n, dout - nu), F32),
        ],
        compiler_params=pltpu.CompilerParams(
            dimension_semantics=("arbitrary",), vmem_limit_bytes=VMEM_LIMIT),
        name="in_proj",
    )(x2, w_bf, b.reshape(1, dout))


CONV_ROWS = 64
CONV_PAD = 32


def _conv_kernel(u_ref, w_ref, cb_ref, g_ref, b_ref, o_ref, pad_ref):
    s = o_ref.shape[0]
    pad_ref[0:CONV_PAD, :] = jnp.zeros((CONV_PAD, D_CONV), F32)

    def glu(i, c):
        r0 = pl.multiple_of(i * 256, 256)
        a = u_ref[pl.ds(r0, 256), 0:D_CONV]
        gate = u_ref[pl.ds(r0, 256), D_CONV:2 * D_CONV]
        pad_ref[pl.ds(CONV_PAD + r0, 256), :] = a * jax.nn.sigmoid(gate)
        return c

    lax.fori_loop(0, s // 256, glu, 0)

    off = CONV_PAD - (CONV_WIDTH - 1)

    def tile(i, c):
        r0 = pl.multiple_of(i * CONV_ROWS, CONV_ROWS)
        acc = jnp.broadcast_to(cb_ref[...], (CONV_ROWS, D_CONV))
        win = pad_ref[pl.ds(r0, CONV_ROWS + CONV_PAD), :]
        for j in range(CONV_WIDTH):
            acc = acc + w_ref[j:j + 1, :] * win[off + j:off + j + CONV_ROWS, :]
        y = _ln(acc, g_ref[...], b_ref[...])
        o_ref[pl.ds(r0, CONV_ROWS), :] = (y * jax.nn.sigmoid(y)).astype(o_ref.dtype)
        return c

    lax.fori_loop(0, s // CONV_ROWS, tile, 0)


def _conv(u, conv_w, conv_b, g, b, batch, seq):
    n = u.shape[0]
    return pl.pallas_call(
        _conv_kernel,
        grid=(batch,),
        in_specs=[
            pl.BlockSpec((seq, 2 * D_CONV), lambda i: (i, 0)),
            pl.BlockSpec((CONV_WIDTH, D_CONV), lambda i: (0, 0)),
            pl.BlockSpec((1, D_CONV), lambda i: (0, 0)),
            pl.BlockSpec((1, D_CONV), lambda i: (0, 0)),
            pl.BlockSpec((1, D_CONV), lambda i: (0, 0)),
        ],
        out_specs=pl.BlockSpec((seq, D_CONV), lambda i: (i, 0)),
        out_shape=jax.ShapeDtypeStruct((n, D_CONV), BF16),
        scratch_shapes=[pltpu.VMEM((seq + CONV_PAD, D_CONV), F32)],
        compiler_params=pltpu.CompilerParams(
            dimension_semantics=("arbitrary",), vmem_limit_bytes=VMEM_LIMIT),
        name="conv",
    )(u, conv_w, conv_b.reshape(1, -1), g.reshape(1, -1), b.reshape(1, -1))


def _attn_kernel(slopes_ref, q_ref, k_ref, v_ref, o_ref, bias_ref, oacc_ref, m_ref, l_ref):
    blk = ATTN_BLOCK
    s_len = q_ref.shape[0]
    hp = pl.program_id(1)
    n_pat = len(DILATED_PATTERNS)

    lane = lax.broadcasted_iota(I32, (blk, LANES), 1)
    head0 = lane < HEAD_DIM
    lane2 = lax.broadcasted_iota(I32, (2 * blk, LANES), 1)
    head0_2 = lane2 < HEAD_DIM

    qi = lax.broadcasted_iota(I32, (blk, 2 * blk), 0)
    kj = lax.broadcasted_iota(I32, (blk, 2 * blk), 1)
    dist = qi + blk - kj
    is_prev = (kj < blk).astype(F32)
    for p, (window, dil) in enumerate(DILATED_PATTERNS):
        n_back = window // dil
        valid = (dist >= 0) & (dist <= n_back)
        dpos = (dist * dil).astype(F32)
        for hh in range(2):
            slope = slopes_ref[2 * hp + hh]
            bias_ref[p, hh] = jnp.where(valid, -slope * dpos, NEG_INF)

    def units(p, dil, specs):
        def rows(ref, start):
            if dil == 1:
                return ref[pl.ds(start, blk), :]
            return ref[pl.ds(start, blk, stride=dil), :]

        scores = []
        values = []
        for cur, prev, first in specs:
            qb = rows(q_ref, cur) * (HEAD_DIM ** -0.5)
            if prev is None:
                kc = rows(k_ref, cur)
                vc = rows(v_ref, cur)
                hmask = head0
            else:
                kc = jnp.concatenate([rows(k_ref, prev), rows(k_ref, cur)], axis=0)
                vc = jnp.concatenate([rows(v_ref, prev), rows(v_ref, cur)], axis=0)
                hmask = head0_2
                pen = jnp.where(first, NEG_INF, 0.0) * is_prev
            kb = kc.astype(BF16)
            for hh in range(2):
                sel = head0 if hh == 0 else jnp.logical_not(head0)
                vsel = hmask if hh == 0 else jnp.logical_not(hmask)
                qh = jnp.where(sel, qb, 0.0).astype(BF16)
                values.append(jnp.where(vsel, vc, 0.0).astype(BF16))
                sc = lax.dot_general(qh, kb, (((1,), (1,)), ((), ())),
                                     preferred_element_type=F32)
                if prev is None:
                    sc = sc + bias_ref[p, hh, :, blk:2 * blk]
                else:
                    sc = sc + bias_ref[p, hh] + pen
                scores.append(sc)
        probs = []
        stats = []
        for sc in scores:
            m = jnp.max(sc, axis=1, keepdims=True)
            e = jnp.exp(sc - m)
            stats.append((m, jnp.sum(e, axis=1, keepdims=True)))
            probs.append(e.astype(BF16))
        outs = [jnp.dot(e, vh, preferred_element_type=F32) for e, vh in zip(probs, values)]
        for n, (cur, prev, first) in enumerate(specs):
            (m0, l0), (m1, l1) = stats[2 * n], stats[2 * n + 1]
            if dil == 1:
                idx = pl.ds(cur, blk)
            else:
                idx = pl.ds(cur, blk, stride=dil)
            oacc_ref[p, idx, :] = outs[2 * n] + outs[2 * n + 1]
            m_ref[p, idx, :] = jnp.where(head0, jnp.broadcast_to(m0, (blk, LANES)),
                                         jnp.broadcast_to(m1, (blk, LANES)))
            l_ref[p, idx, :] = jnp.where(head0, jnp.broadcast_to(l0, (blk, LANES)),
                                         jnp.broadcast_to(l1, (blk, LANES)))

    for p, (window, dil) in enumerate(DILATED_PATTERNS):
        stream_len = s_len // dil
        nb = stream_len // blk

        if nb == 1:
            def body(r4, c, p=p, dil=dil):
                units(p, dil, [(r4 * ATTN_UNROLL + uu, None, None) for uu in range(ATTN_UNROLL)])
                return c
            lax.fori_loop(0, dil // ATTN_UNROLL, body, 0)
        else:
            def body(u4, c, p=p, dil=dil, nb=nb):
                specs = []
                for uu in range(ATTN_UNROLL):
                    u = u4 * ATTN_UNROLL + uu
                    r = u // nb
                    i = u - r * nb
                    cur = r + i * (blk * dil)
                    prev = r + jnp.maximum(i - 1, 0) * (blk * dil)
                    if dil == 1:
                        cur = pl.multiple_of(cur, blk)
                        prev = pl.multiple_of(prev, blk)
                    specs.append((cur, prev, i == 0))
                units(p, dil, specs)
                return c
            lax.fori_loop(0, dil * nb // ATTN_UNROLL, body, 0)

    def merge(i, c):
        r0 = pl.multiple_of(i * 256, 256)
        ms = [m_ref[p, pl.ds(r0, 256), :] for p in range(n_pat)]
        mx = functools.reduce(jnp.maximum, ms)
        num = None
        den = None
        for p in range(n_pat):
            w = jnp.exp(ms[p] - mx)
            tn = w * oacc_ref[p, pl.ds(r0, 256), :]
            td = w * l_ref[p, pl.ds(r0, 256), :]
            num = tn if num is None else num + tn
            den = td if den is None else den + td
        o_ref[pl.ds(r0, 256), :] = (num / den).astype(o_ref.dtype)
        return c

    lax.fori_loop(0, s_len // 256, merge, 0)


def _attention(qkv, slopes, batch, seq):
    n = qkv.shape[0]
    n_hp = D_ATTN // LANES
    n_pat = len(DILATED_PATTERNS)
    return pl.pallas_call(
        _attn_kernel,
        grid=(batch, n_hp),
        in_specs=[
            pl.BlockSpec(memory_space=pltpu.SMEM),
            pl.BlockSpec((seq, LANES), lambda b, h: (b, h)),
            pl.BlockSpec((seq, LANES), lambda b, h: (b, n_hp + h)),
            pl.BlockSpec((seq, LANES), lambda b, h: (b, 2 * n_hp + h)),
        ],
        out_specs=pl.BlockSpec((seq, LANES), lambda b, h: (b, h)),
        out_shape=jax.ShapeDtypeStruct((n, D_ATTN), BF16),
        scratch_shapes=[
            pltpu.VMEM((n_pat, 2, ATTN_BLOCK, 2 * ATTN_BLOCK), F32),
            pltpu.VMEM((n_pat, seq, LANES), F32),
            pltpu.VMEM((n_pat, seq, LANES), F32),
            pltpu.VMEM((n_pat, seq, LANES), F32),
        ],
        compiler_params=pltpu.CompilerParams(
            dimension_semantics=("arbitrary", "arbitrary"), vmem_limit_bytes=VMEM_LIMIT),
        name="attn",
    )(slopes, qkv, qkv, qkv)


def _out_proj_kernel(alpha, yc_ref, ya_ref, x_ref, wc_ref, wa_ref, b_ref, g_ref, be_ref, o_ref):
    y = jnp.dot(yc_ref[...], wc_ref[...], preferred_element_type=F32)
    y = y + jnp.dot(ya_ref[...], wa_ref[...], preferred_element_type=F32)
    y = y + b_ref[...]
    o_ref[...] = _ln(alpha * x_ref[...] + y, g_ref[...], be_ref[...])


def _out_proj(yc, ya, x2, w_out_bf, b_out, g, b, alpha, tm=256):
    n, d = x2.shape
    row = lambda i: (i, 0)
    fix = lambda i: (0, 0)
    return pl.pallas_call(
        functools.partial(_out_proj_kernel, alpha),
        grid=(n // tm,),
        in_specs=[
            pl.BlockSpec((tm, D_CONV), row),
            pl.BlockSpec((tm, D_ATTN), row),
            pl.BlockSpec((tm, d), row),
            pl.BlockSpec((D_CONV, d), fix),
            pl.BlockSpec((D_ATTN, d), fix),
            pl.BlockSpec((1, d), fix),
            pl.BlockSpec((1, d), fix),
            pl.BlockSpec((1, d), fix),
        ],
        out_specs=pl.BlockSpec((tm, d), row),
        out_shape=jax.ShapeDtypeStruct((n, d), F32),
        compiler_params=pltpu.CompilerParams(
            dimension_semantics=("arbitrary",), vmem_limit_bytes=VMEM_LIMIT),
        name="out_proj",
    )(yc, ya, x2, w_out_bf[:D_CONV], w_out_bf[D_CONV:], b_out.reshape(1, d),
      g.reshape(1, d), b.reshape(1, d))


def _route_fns(q_scr, keys_ref, gt_scr, it_scr):
    k_top = PEER_TOPK
    nk = PEER_NKEYS
    lt = LANES
    iota_n = lax.broadcasted_iota(I32, (nk, lt), 0)
    iota_k = lax.broadcasted_iota(I32, (k_top, lt), 0)
    iota_8 = lax.broadcasted_iota(I32, (SUBLANES, lt), 0)
    neg = -jnp.inf

    half = k_top // 2
    blocks = ([(0, 0), (0, half)] + [(i, 0) for i in range(1, half)])
    pos = jnp.concatenate(
        [iota_8 + (i * k_top + j0) for (i, j0) in blocks] + [(iota_8 + half) * k_top], axis=0)

    def sub_topk(h):
        top_s = []
        top_i = []
        for p in range(2):
            col = pl.multiple_of(h * PEER_QDIM + p * (PEER_QDIM // 2), PEER_QDIM // 2)
            q = q_scr[:, pl.ds(col, PEER_QDIM // 2)].astype(BF16)
            s = lax.dot_general(keys_ref[h, p], q, (((1,), (1,)), ((), ())),
                                preferred_element_type=F32)
            ts = jnp.zeros((k_top, lt), F32)
            ti = jnp.zeros((k_top, lt), I32)
            for k in range(k_top):
                m = jnp.max(s, axis=0, keepdims=True)
                am = jnp.min(jnp.where(s == m, iota_n, nk), axis=0, keepdims=True)
                s = jnp.where(iota_n == am, neg, s)
                ts = jnp.where(iota_k == k, m, ts)
                ti = jnp.where(iota_k == k, am, ti)
            top_s.append(ts)
            top_i.append(ti)
        return top_s[0], top_s[1], top_i[0], top_i[1]

    def combine(h, tops):
        sa, sb, ia, ib = tops
        cand = jnp.concatenate(
            [sa[i:i + 1, :] + sb[j0:j0 + SUBLANES, :] for (i, j0) in blocks]
            + [sa[half:, :] + sb[0:1, :]], axis=0)
        eid = jnp.concatenate(
            [ia[i:i + 1, :] * nk + ib[j0:j0 + SUBLANES, :] for (i, j0) in blocks]
            + [ia[half:, :] * nk + ib[0:1, :]], axis=0)
        bs = jnp.zeros((k_top, lt), F32)
        be = jnp.zeros((k_top, lt), I32)
        for k in range(k_top):
            m = jnp.max(cand, axis=0, keepdims=True)
            am = jnp.min(jnp.where(cand == m, pos, k_top * k_top), axis=0, keepdims=True)
            sel = pos == am
            e = jnp.max(jnp.where(sel, eid, -1), axis=0, keepdims=True)
            cand = jnp.where(sel, neg, cand)
            bs = jnp.where(iota_k == k, m, bs)
            be = jnp.where(iota_k == k, e, be)
        ex = jnp.exp(bs - bs[0:1, :])
        gate = ex / jnp.sum(ex, axis=0, keepdims=True)
        r0 = pl.multiple_of(h * k_top, k_top)
        gt_scr[pl.ds(r0, k_top), :] = gate
        it_scr[pl.ds(r0, k_top), :] = be

    return sub_topk, combine


def _peer_kernel(alpha, h_ref, h1_ref, h2_ref, wq_ref, keys_ref, uv_ref, lg_ref, lb_ref, o_ref,
                 *scratch):
    bufs = scratch[:PEER_ALLOCS]
    (sem, y_scr, acc_scr, wb_scr, q_scr, gt_scr, it_scr, g_ring, idx_stage, idx_ring,
     ring_sem) = scratch[PEER_ALLOCS:]
    tb, d = h_ref.shape
    hk = PEER_HEADS * PEER_TOPK
    n_lt = d // LANES
    n_kg = hk // SUBLANES
    st = PEER_SET
    ahead_sets = PEER_AHEAD // st
    step = pl.program_id(0)
    cur = step % PEER_RING
    sub_topk, combine = _route_fns(q_scr, keys_ref, gt_scr, it_scr)

    def route_queries(src_ref):
        q_scr[...] = jnp.dot(src_ref[...].astype(BF16), wq_ref[...], preferred_element_type=F32)

    def route_publish(slot):
        g_ring[slot] = gt_scr[...].T
        idx_stage[...] = it_scr[...].T
        cp = pltpu.make_async_copy(idx_stage, idx_ring.at[slot], ring_sem.at[0])
        cp.start()
        cp.wait()

    def route_block(src_ref, slot):
        route_queries(src_ref)

        def one(h, c):
            combine(h, sub_topk(h))
            return c
        lax.fori_loop(0, PEER_HEADS, one, 0)
        route_publish(slot)

    def issue_group(tok, alloc, slot, kg):
        la = tok + PEER_AHEAD
        over = la >= tb
        ring = jnp.where(over, (step + 1) % PEER_RING, cur)
        row = jnp.where(over, la - tb, la)
        for r in range(SUBLANES):
            e = idx_ring[ring, row, kg * SUBLANES + r]
            pltpu.make_async_copy(
                uv_ref.at[e], bufs[alloc].at[slot, kg, :, r, :],
                sem.at[alloc, slot]).start(priority=r % 2)

    def wait(alloc, slot):
        pltpu.make_async_copy(
            bufs[alloc].at[slot], bufs[alloc].at[slot], sem.at[alloc, slot]).wait()

    eye = (lax.broadcasted_iota(I32, (hk, hk), 0) == lax.broadcasted_iota(I32, (hk, hk), 1))
    ones_row = jnp.ones((SUBLANES, LANES), BF16)
    ones_sq = jnp.ones((hk, LANES), BF16)
    sub = lax.broadcasted_iota(I32, (SUBLANES, LANES), 0)

    def gate_chain(tok0):
        acc = acc_scr[...].reshape(st * hk, LANES).astype(BF16)
        sums = lax.dot_general(ones_row, acc, (((1,), (1,)), ((), ())),
                               preferred_element_type=F32)
        a = jnp.zeros((st, hk), F32)
        for j in range(st):
            a = jnp.where(sub == j, sums[:, j * hk:(j + 1) * hk], a)
        act = 0.5 * a * (1.0 + lax.erf(a * (2.0 ** -0.5)))
        w = act * g_ring[cur, pl.ds(tok0, st), :]
        diag = jnp.concatenate(
            [jnp.where(eye, jnp.broadcast_to(w[j:j + 1, :], (hk, hk)), 0.0).astype(BF16)
             for j in range(st)], axis=0)
        wb = jnp.dot(diag, ones_sq, preferred_element_type=F32)
        wb_scr[...] = wb.reshape(st, hk, LANES)

    @pl.when(step == 0)
    def _():
        route_block(h_ref, 0)
        route_block(h1_ref, 1)
        for a in range(ahead_sets):
            def first(j, c, a=a):
                for kg in range(n_kg):
                    issue_group(a * st + j - PEER_AHEAD, a, j, kg)
                return c
            lax.fori_loop(0, st, first, 0)

    route_queries(h2_ref)
    heads_per_body = PEER_HEADS // (tb // (PEER_ALLOCS * st))

    body_tokens = PEER_ALLOCS * st

    def body(it, carry):
        base = pl.multiple_of(it * body_tokens, body_tokens)
        tops = None
        for q in range(PEER_ALLOCS):
            buf = bufs[q]
            nxt = (q + ahead_sets) % PEER_ALLOCS
            tok0 = base + q * st
            head = it * heads_per_body + q // 2
            if q % 2 == 0:
                tops = sub_topk(head)
            else:
                combine(head, tops)
            for j in range(st):
                wait(q, j)

            def loop_a(i, c, buf=buf, nxt=nxt, tok0=tok0):
                for j in range(st):
                    issue_group(tok0 + j, nxt, j, i)
                    x_row = h_ref[pl.ds(tok0 + j, 1), :]
                    for g2 in range(2):
                        kg = 2 * i + g2
                        acc = None
                        for c2 in range(n_lt):
                            xc = jnp.broadcast_to(x_row[:, c2 * LANES:(c2 + 1) * LANES],
                                                  (SUBLANES, LANES))
                            term = buf[j, kg, c2, :, :] * xc
                            acc = term if acc is None else acc + term
                        r0 = pl.multiple_of(kg * SUBLANES, SUBLANES)
                        acc_scr[j, pl.ds(r0, SUBLANES), :] = acc
                return c

            lax.fori_loop(0, n_kg // 2, loop_a, 0, unroll=True)

            gate_chain(pl.multiple_of(tok0, st))

            def loop_b(c2, c, buf=buf, nxt=nxt, tok0=tok0):
                for j in range(st):
                    issue_group(tok0 + j, nxt, j, n_kg // 2 + c2)
                tile = jnp.zeros((SUBLANES, LANES), F32)
                for j in range(st):
                    wb = wb_scr[j].reshape(n_kg, SUBLANES, LANES)
                    prod = buf[j, :, n_lt + c2, :, :] * wb
                    row = jnp.sum(jnp.sum(prod, axis=0), axis=0, keepdims=True)
                    tile = jnp.where(sub == j, jnp.broadcast_to(row, (SUBLANES, LANES)), tile)
                col = pl.multiple_of(c2 * LANES, LANES)
                y_scr[pl.ds(pl.multiple_of(tok0, st), st), pl.ds(col, LANES)] = tile
                return c

            lax.fori_loop(0, n_lt, loop_b, 0, unroll=True)
        return carry

    lax.fori_loop(0, tb // body_tokens, body, 0)
    route_publish((step + 2) % PEER_RING)

    @pl.when(step == pl.num_programs(0) - 1)
    def _():
        for t in range(PEER_AHEAD):
            wait((t // st) % PEER_ALLOCS, t % st)

    o_ref[...] = _ln(alpha * h_ref[...] + y_scr[...], lg_ref[...], lb_ref[...])


def _peer(h1, wq_bf, keys_bf, uv3, g, b, alpha):
    n, d = h1.shape
    tb = PEER_TOKENS
    hk = PEER_HEADS * PEER_TOPK
    assert SUBLANES == PEER_SET and PEER_AHEAD % PEER_SET == 0
    assert 0 < PEER_AHEAD // PEER_SET < PEER_ALLOCS - 1 and PEER_AHEAD <= tb
    assert hk // SUBLANES == 2 * (d // LANES) and tb == LANES and hk == LANES
    assert tb % (PEER_ALLOCS * PEER_SET) == 0 and n % tb == 0
    assert PEER_ALLOCS * (tb // (PEER_ALLOCS * PEER_SET)) == 2 * PEER_HEADS
    last = n // tb - 1
    fix = lambda i: (0, 0)
    slot_shape = (PEER_SET, hk // SUBLANES, uv3.shape[1], SUBLANES, LANES)
    return pl.pallas_call(
        functools.partial(_peer_kernel, alpha),
        grid=(n // tb,),
        in_specs=[
            pl.BlockSpec((tb, d), lambda i: (i, 0)),
            pl.BlockSpec((tb, d), lambda i: (jnp.minimum(i + 1, last), 0)),
            pl.BlockSpec((tb, d), lambda i: (jnp.minimum(i + 2, last), 0)),
            pl.BlockSpec(wq_bf.shape, fix),
            pl.BlockSpec(keys_bf.shape, lambda i: (0, 0, 0, 0)),
            pl.BlockSpec(memory_space=pl.ANY),
            pl.BlockSpec((1, d), fix),
            pl.BlockSpec((1, d), fix),
        ],
        out_specs=pl.BlockSpec((tb, d), lambda i: (i, 0)),
        out_shape=jax.ShapeDtypeStruct((n, d), F32),
        scratch_shapes=(
            [pltpu.VMEM(slot_shape, F32) for _ in range(PEER_ALLOCS)]
            + [pltpu.SemaphoreType.DMA((PEER_ALLOCS, PEER_SET)),
               pltpu.VMEM((tb, d), F32),
               pltpu.VMEM((PEER_SET, hk, LANES), F32),
               pltpu.VMEM((PEER_SET, hk, LANES), F32),
               pltpu.VMEM((tb, PEER_HEADS * PEER_QDIM), F32),
               pltpu.VMEM((hk, tb), F32),
               pltpu.VMEM((hk, tb), I32),
               pltpu.VMEM((PEER_RING, tb, hk), F32),
               pltpu.VMEM((tb, hk), I32),
               pltpu.SMEM((PEER_RING, tb, hk), I32),
               pltpu.SemaphoreType.DMA((1,))]),
        compiler_params=pltpu.CompilerParams(
            dimension_semantics=("arbitrary",), vmem_limit_bytes=VMEM_LIMIT),
        name="peer",
    )(h1, h1, h1, wq_bf, keys_bf, uv3, g.reshape(1, d), b.reshape(1, d))


def kernel(x, w_in, b_in, conv_w, conv_b, conv_ln_g, conv_ln_b, w_out, b_out, ln1_g, ln1_b,
           peer_wq, peer_keys, peer_u, peer_v, ln2_g, ln2_b):
    batch, seq, d = x.shape
    depth = w_in.shape[0]
    alpha = (2.0 * depth) ** 0.25
    slopes = 2.0 ** (-8.0 * jnp.arange(1, N_HEADS + 1, dtype=F32) / N_HEADS)
    h = x.reshape(batch * seq, d)
    for l in range(depth):
        u, qkv = _in_proj(h, w_in[l].astype(BF16), b_in[l])
        yc = _conv(u, conv_w[l], conv_b[l], conv_ln_g[l], conv_ln_b[l], batch, seq)
        ya = _attention(qkv, slopes, batch, seq)
        h1 = _out_proj(yc, ya, h, w_out[l].astype(BF16), b_out[l], ln1_g[l], ln1_b[l], alpha)
        uv3 = jnp.concatenate([peer_u[l], peer_v[l]], axis=1).reshape(-1, 2 * d // LANES, LANES)
        h = _peer(h1, peer_wq[l].astype(BF16), peer_keys[l].astype(BF16), uv3,
                  ln2_g[l], ln2_b[l], alpha)
    return h.reshape(batch, seq, d)
```

```python
import functools
import math

import jax
import jax.numpy as jnp
from jax import lax
from jax.experimental import pallas as pl
from jax.experimental.pallas import tpu as pltpu

F32 = jnp.float32
BF16 = jnp.bfloat16
I32 = jnp.int32

D_MODEL = 1024
D_CONV = D_MODEL // 2
D_ATTN = D_MODEL - D_CONV
HEAD_DIM = 64
N_HEADS = D_ATTN // HEAD_DIM
CONV_WIDTH = 31
DILATED_PATTERNS = ((128, 1), (512, 4), (2048, 16))
ATTN_BLOCK = 128
ATTN_UNROLL = 4
PEER_HEADS = 8
PEER_NKEYS = 128
PEER_QDIM = 256
PEER_TOPK = 16
LN_EPS = 1e-5
NEG_INF = -1e30

LANES = 128
SUBLANES = 8
VMEM_LIMIT = 56 * 1024 * 1024

PEER_SET = 8
PEER_ALLOCS = 4
PEER_AHEAD = 2 * PEER_SET
PEER_TOKENS = 128
PEER_RING = 3


def _ln(x, g, b):
    mu = jnp.mean(x, axis=-1, keepdims=True)
    xc = x - mu
    var = jnp.mean(xc * xc, axis=-1, keepdims=True)
    return xc * lax.rsqrt(var + LN_EPS) * g + b


def _in_proj_kernel(x_ref, w_ref, b_ref, u_ref, qkv_ref):
    x = x_ref[...].astype(BF16)
    nu = u_ref.shape[1]
    u_ref[...] = jnp.dot(x, w_ref[:, :nu], preferred_element_type=F32) + b_ref[:, :nu]
    qkv_ref[...] = jnp.dot(x, w_ref[:, nu:], preferred_element_type=F32) + b_ref[:, nu:]


def _in_proj(x2, w_bf, b, tm=256):
    n, d = x2.shape
    dout = w_bf.shape[1]
    nu = 2 * D_CONV
    return pl.pallas_call(
        _in_proj_kernel,
        grid=(n // tm,),
        in_specs=[
            pl.BlockSpec((tm, d), lambda i: (i, 0)),
            pl.BlockSpec((d, dout), lambda i: (0, 0)),
            pl.BlockSpec((1, dout), lambda i: (0, 0)),
        ],
        out_specs=[
            pl.BlockSpec((tm, nu), lambda i: (i, 0)),
            pl.BlockSpec((tm, dout - nu), lambda i: (i, 0)),
        ],
        out_shape=[
            jax.ShapeDtypeStruct((n, nu), F32),
            jax.ShapeDtypeStruct((n, dout - nu), F32),
        ],
        compiler_params=pltpu.CompilerParams(
            dimension_semantics=("arbitrary",), vmem_limit_bytes=VMEM_LIMIT),
        name="in_proj",
    )(x2, w_bf, b.reshape(1, dout))


CONV_ROWS = 64
CONV_PAD = 32


def _conv_kernel(u_ref, w_ref, cb_ref, g_ref, b_ref, o_ref, pad_ref):
    s = o_ref.shape[0]
    pad_ref[0:CONV_PAD, :] = jnp.zeros((CONV_PAD, D_CONV), F32)

    def glu(i, c):
        r0 = pl.multiple_of(i * 256, 256)
        a = u_ref[pl.ds(r0, 256), 0:D_CONV]
        gate = u_ref[pl.ds(r0, 256), D_CONV:2 * D_CONV]
        pad_ref[pl.ds(CONV_PAD + r0, 256), :] = a * jax.nn.sigmoid(gate)
        return c

    lax.fori_loop(0, s // 256, glu, 0)

    off = CONV_PAD - (CONV_WIDTH - 1)

    def tile(i, c):
        r0 = pl.multiple_of(i * CONV_ROWS, CONV_ROWS)
        acc = jnp.broadcast_to(cb_ref[...], (CONV_ROWS, D_CONV))
        win = pad_ref[pl.ds(r0, CONV_ROWS + CONV_PAD), :]
        for j in range(CONV_WIDTH):
            acc = acc + w_ref[j:j + 1, :] * win[off + j:off + j + CONV_ROWS, :]
        y = _ln(acc, g_ref[...], b_ref[...])
        o_ref[pl.ds(r0, CONV_ROWS), :] = (y * jax.nn.sigmoid(y)).astype(o_ref.dtype)
        return c

    lax.fori_loop(0, s // CONV_ROWS, tile, 0)


def _conv(u, conv_w, conv_b, g, b, batch, seq):
    n = u.shape[0]
    return pl.pallas_call(
        _conv_kernel,
        grid=(batch,),
        in_specs=[
            pl.BlockSpec((seq, 2 * D_CONV), lambda i: (i, 0)),
            pl.BlockSpec((CONV_WIDTH, D_CONV), lambda i: (0, 0)),
            pl.BlockSpec((1, D_CONV), lambda i: (0, 0)),
            pl.BlockSpec((1, D_CONV), lambda i: (0, 0)),
            pl.BlockSpec((1, D_CONV), lambda i: (0, 0)),
        ],
        out_specs=pl.BlockSpec((seq, D_CONV), lambda i: (i, 0)),
        out_shape=jax.ShapeDtypeStruct((n, D_CONV), BF16),
        scratch_shapes=[pltpu.VMEM((seq + CONV_PAD, D_CONV), F32)],
        compiler_params=pltpu.CompilerParams(
            dimension_semantics=("arbitrary",), vmem_limit_bytes=VMEM_LIMIT),
        name="conv",
    )(u, conv_w, conv_b.reshape(1, -1), g.reshape(1, -1), b.reshape(1, -1))


def _attn_kernel(slopes_ref, q_ref, k_ref, v_ref, o_ref, bias_ref, oacc_ref, m_ref, l_ref):
    blk = ATTN_BLOCK
    s_len = q_ref.shape[0]
    hp = pl.program_id(1)
    n_pat = len(DILATED_PATTERNS)

    lane = lax.broadcasted_iota(I32, (blk, LANES), 1)
    head0 = lane < HEAD_DIM
    lane2 = lax.broadcasted_iota(I32, (2 * blk, LANES), 1)
    head0_2 = lane2 < HEAD_DIM

    qi = lax.broadcasted_iota(I32, (blk, 2 * blk), 0)
    kj = lax.broadcasted_iota(I32, (blk, 2 * blk), 1)
    dist = qi + blk - kj
    is_prev = (kj < blk).astype(F32)
    for p, (window, dil) in enumerate(DILATED_PATTERNS):
        n_back = window // dil
        valid = (dist >= 0) & (dist <= n_back)
        dpos = (dist * dil).astype(F32)
        for hh in range(2):
            slope = slopes_ref[2 * hp + hh]
            bias_ref[p, hh] = jnp.where(valid, -slope * dpos, NEG_INF)

    def units(p, dil, specs):
        def rows(ref, start):
            if dil == 1:
                return ref[pl.ds(start, blk), :]
            return ref[pl.ds(start, blk, stride=dil), :]

        scores = []
        values = []
        for cur, prev, first in specs:
            qb = rows(q_ref, cur) * (HEAD_DIM ** -0.5)
            if prev is None:
                kc = rows(k_ref, cur)
                vc = rows(v_ref, cur)
                hmask = head0
            else:
                kc = jnp.concatenate([rows(k_ref, prev), rows(k_ref, cur)], axis=0)
                vc = jnp.concatenate([rows(v_ref, prev), rows(v_ref, cur)], axis=0)
                hmask = head0_2
                pen = jnp.where(first, NEG_INF, 0.0) * is_prev
            kb = kc.astype(BF16)
            for hh in range(2):
                sel = head0 if hh == 0 else jnp.logical_not(head0)
                vsel = hmask if hh == 0 else jnp.logical_not(hmask)
                qh = jnp.where(sel, qb, 0.0).astype(BF16)
                values.append(jnp.where(vsel, vc, 0.0).astype(BF16))
                sc = lax.dot_general(qh, kb, (((1,), (1,)), ((), ())),
                                     preferred_element_type=F32)
                if prev is None:
                    sc = sc + bias_ref[p, hh, :, blk:2 * blk]
                else:
                    sc = sc + bias_ref[p, hh] + pen
                scores.append(sc)
        probs = []
        stats = []
        for sc in scores:
            m = jnp.max(sc, axis=1, keepdims=True)
            e = jnp.exp(sc - m)
            stats.append((m, jnp.sum(e, axis=1, keepdims=True)))
            probs.append(e.astype(BF16))
        outs = [jnp.dot(e, vh, preferred_element_type=F32) for e, vh in zip(probs, values)]
        for n, (cur, prev, first) in enumerate(specs):
            (m0, l0), (m1, l1) = stats[2 * n], stats[2 * n + 1]
            if dil == 1:
                idx = pl.ds(cur, blk)
            else:
                idx = pl.ds(cur, blk, stride=dil)
            oacc_ref[p, idx, :] = outs[2 * n] + outs[2 * n + 1]
            m_ref[p, idx, :] = jnp.where(head0, jnp.broadcast_to(m0, (blk, LANES)),
                                         jnp.broadcast_to(m1, (blk, LANES)))
            l_ref[p, idx, :] = jnp.where(head0, jnp.broadcast_to(l0, (blk, LANES)),
                                         jnp.broadcast_to(l1, (blk, LANES)))

    for p, (window, dil) in enumerate(DILATED_PATTERNS):
        stream_len = s_len // dil
        nb = stream_len // blk

        if nb == 1:
            def body(r4, c, p=p, dil=dil):
                units(p, dil, [(r4 * ATTN_UNROLL + uu, None, None) for uu in range(ATTN_UNROLL)])
                return c
            lax.fori_loop(0, dil // ATTN_UNROLL, body, 0)
        else:
            def body(u4, c, p=p, dil=dil, nb=nb):
                specs = []
                for uu in range(ATTN_UNROLL):
                    u = u4 * ATTN_UNROLL + uu
                    r = u // nb
                    i = u - r * nb
                    cur = r + i * (blk * dil)
                    prev = r + jnp.maximum(i - 1, 0) * (blk * dil)
                    if dil == 1:
                        cur = pl.multiple_of(cur, blk)
                        prev = pl.multiple_of(prev, blk)
                    specs.append((cur, prev, i == 0))
                units(p, dil, specs)
                return c
            lax.fori_loop(0, dil * nb // ATTN_UNROLL, body, 0)

    def merge(i, c):
        r0 = pl.multiple_of(i * 256, 256)
        ms = [m_ref[p, pl.ds(r0, 256), :] for p in range(n_pat)]
        mx = functools.reduce(jnp.maximum, ms)
        num = None
        den = None
        for p in range(n_pat):
            w = jnp.exp(ms[p] - mx)
            tn = w * oacc_ref[p, pl.ds(r0, 256), :]
            td = w * l_ref[p, pl.ds(r0, 256), :]
            num = tn if num is None else num + tn
            den = td if den is None else den + td
        o_ref[pl.ds(r0, 256), :] = (num / den).astype(o_ref.dtype)
        return c

    lax.fori_loop(0, s_len // 256, merge, 0)


def _attention(qkv, slopes, batch, seq):
    n = qkv.shape[0]
    n_hp = D_ATTN // LANES
    n_pat = len(DILATED_PATTERNS)
    return pl.pallas_call(
        _attn_kernel,
        grid=(batch, n_hp),
        in_specs=[
            pl.BlockSpec(memory_space=pltpu.SMEM),
            pl.BlockSpec((seq, LANES), lambda b, h: (b, h)),
            pl.BlockSpec((seq, LANES), lambda b, h: (b, n_hp + h)),
            pl.BlockSpec((seq, LANES), lambda b, h: (b, 2 * n_hp + h)),
        ],
        out_specs=pl.BlockSpec((seq, LANES), lambda b, h: (b, h)),
        out_shape=jax.ShapeDtypeStruct((n, D_ATTN), BF16),
        scratch_shapes=[
            pltpu.VMEM((n_pat, 2, ATTN_BLOCK, 2 * ATTN_BLOCK), F32),
            pltpu.VMEM((n_pat, seq, LANES), F32),
            pltpu.VMEM((n_pat, seq, LANES), F32),
            pltpu.VMEM((n_pat, seq, LANES), F32),
        ],
        compiler_params=pltpu.CompilerParams(
            dimension_semantics=("arbitrary", "arbitrary"), vmem_limit_bytes=VMEM_LIMIT),
        name="attn",
    )(slopes, qkv, qkv, qkv)


def _out_proj_kernel(alpha, yc_ref, ya_ref, x_ref, wc_ref, wa_ref, b_ref, g_ref, be_ref, o_ref):
    y = jnp.dot(yc_ref[...], wc_ref[...], preferred_element_type=F32)
    y = y + jnp.dot(ya_ref[...], wa_ref[...], preferred_element_type=F32)
    y = y + b_ref[...]
    o_ref[...] = _ln(alpha * x_ref[...] + y, g_ref[...], be_ref[...])


def _out_proj(yc, ya, x2, w_out_bf, b_out, g, b, alpha, tm=256):
    n, d = x2.shape
    row = lambda i: (i, 0)
    fix = lambda i: (0, 0)
    return pl.pallas_call(
        functools.partial(_out_proj_kernel, alpha),
        grid=(n // tm,),
        in_specs=[
            pl.BlockSpec((tm, D_CONV), row),
            pl.BlockSpec((tm, D_ATTN), row),
            pl.BlockSpec((tm, d), row),
            pl.BlockSpec((D_CONV, d), fix),
            pl.BlockSpec((D_ATTN, d), fix),
            pl.BlockSpec((1, d), fix),
            pl.BlockSpec((1, d), fix),
            pl.BlockSpec((1, d), fix),
        ],
        out_specs=pl.BlockSpec((tm, d), row),
        out_shape=jax.ShapeDtypeStruct((n, d), F32),
        compiler_params=pltpu.CompilerParams(
            dimension_semantics=("arbitrary",), vmem_limit_bytes=VMEM_LIMIT),
        name="out_proj",
    )(yc, ya, x2, w_out_bf[:D_CONV], w_out_bf[D_CONV:], b_out.reshape(1, d),
      g.reshape(1, d), b.reshape(1, d))


def _route_fns(hb_scr, wq_ref, keys_ref, gt_scr, it_scr):
    k_top = PEER_TOPK
    nk = PEER_NKEYS
    lt = LANES
    iota_n = lax.broadcasted_iota(I32, (nk, lt), 0)
    iota_k = lax.broadcasted_iota(I32, (k_top, lt), 0)
    iota_8 = lax.broadcasted_iota(I32, (SUBLANES, lt), 0)
    neg = -jnp.inf

    half = k_top // 2
    blocks = ([(0, 0), (0, half)] + [(i, 0) for i in range(1, half)])
    pos = jnp.concatenate(
        [iota_8 + (i * k_top + j0) for (i, j0) in blocks] + [(iota_8 + half) * k_top], axis=0)

    def sub_topk(h):
        top_s = []
        top_i = []
        for p in range(2):
            col = pl.multiple_of(h * PEER_QDIM + p * (PEER_QDIM // 2), PEER_QDIM // 2)
            q = jnp.dot(hb_scr[...], wq_ref[:, pl.ds(col, PEER_QDIM // 2)],
                        preferred_element_type=F32).astype(BF16)
            s = lax.dot_general(keys_ref[h, p], q, (((1,), (1,)), ((), ())),
                                preferred_element_type=F32)
            ts = jnp.zeros((k_top, lt), F32)
            ti = jnp.zeros((k_top, lt), I32)
            for k in range(k_top):
                m = jnp.max(s, axis=0, keepdims=True)
                am = jnp.min(jnp.where(s == m, iota_n, nk), axis=0, keepdims=True)
                s = jnp.where(iota_n == am, neg, s)
                ts = jnp.where(iota_k == k, m, ts)
                ti = jnp.where(iota_k == k, am, ti)
            top_s.append(ts)
            top_i.append(ti)
        return top_s[0], top_s[1], top_i[0], top_i[1]

    def combine(h, tops):
        sa, sb, ia, ib = tops
        cand = jnp.concatenate(
            [sa[i:i + 1, :] + sb[j0:j0 + SUBLANES, :] for (i, j0) in blocks]
            + [sa[half:, :] + sb[0:1, :]], axis=0)
        eid = jnp.concatenate(
            [ia[i:i + 1, :] * nk + ib[j0:j0 + SUBLANES, :] for (i, j0) in blocks]
            + [ia[half:, :] * nk + ib[0:1, :]], axis=0)
        bs = jnp.zeros((k_top, lt), F32)
        be = jnp.zeros((k_top, lt), I32)
        for k in range(k_top):
            m = jnp.max(cand, axis=0, keepdims=True)
            am = jnp.min(jnp.where(cand == m, pos, k_top * k_top), axis=0, keepdims=True)
            sel = pos == am
            e = jnp.max(jnp.where(sel, eid, -1), axis=0, keepdims=True)
            cand = jnp.where(sel, neg, cand)
            bs = jnp.where(iota_k == k, m, bs)
            be = jnp.where(iota_k == k, e, be)
        ex = jnp.exp(bs - bs[0:1, :])
        gate = ex / jnp.sum(ex, axis=0, keepdims=True)
        r0 = pl.multiple_of(h * k_top, k_top)
        gt_scr[pl.ds(r0, k_top), :] = gate
        it_scr[pl.ds(r0, k_top), :] = be

    return sub_topk, combine


def _peer_kernel(alpha, h_ref, h1_ref, h2_ref, wq_ref, keys_ref, uv_ref, lg_ref, lb_ref, o_ref,
                 *scratch):
    bufs = scratch[:PEER_ALLOCS]
    (sem, y_scr, acc_scr, wb_scr, hb_scr, gt_scr, it_scr, g_ring, idx_stage, idx_ring,
     ring_sem) = scratch[PEER_ALLOCS:]
    tb, d = h_ref.shape
    hk = PEER_HEADS * PEER_TOPK
    n_lt = d // LANES
    n_kg = hk // SUBLANES
    st = PEER_SET
    ahead_sets = PEER_AHEAD // st
    step = pl.program_id(0)
    cur = step % PEER_RING
    sub_topk, combine = _route_fns(hb_scr, wq_ref, keys_ref, gt_scr, it_scr)

    def route_source(src_ref):
        hb_scr[...] = src_ref[...].astype(BF16)

    def ring_copy(slot):
        return pltpu.make_async_copy(idx_stage, idx_ring.at[slot], ring_sem.at[0])

    def route_publish(slot):
        g_ring[slot] = gt_scr[...].T
        idx_stage[...] = it_scr[...].T
        ring_copy(slot).start()

    def route_block(src_ref, slot):
        route_source(src_ref)

        def one(h, c):
            combine(h, sub_topk(h))
            return c
        lax.fori_loop(0, PEER_HEADS, one, 0)
        route_publish(slot)
        ring_copy(slot).wait()

    def issue_group(tok, alloc, slot, kg):
        la = tok + PEER_AHEAD
        over = la >= tb
        ring = jnp.where(over, (step + 1) % PEER_RING, cur)
        row = jnp.where(over, la - tb, la)
        for r in range(SUBLANES):
            e = idx_ring[ring, row, kg * SUBLANES + r]
            pltpu.make_async_copy(
                uv_ref.at[e], bufs[alloc].at[slot, kg, :, r, :],
                sem.at[alloc, slot]).start(priority=r % 2)

    def wait(alloc, slot):
        pltpu.make_async_copy(
            bufs[alloc].at[slot], bufs[alloc].at[slot], sem.at[alloc, slot]).wait()

    eye = (lax.broadcasted_iota(I32, (hk, hk), 0) == lax.broadcasted_iota(I32, (hk, hk), 1))
    ones_row = jnp.ones((SUBLANES, LANES), BF16)
    ones_sq = jnp.ones((hk, LANES), BF16)
    sub = lax.broadcasted_iota(I32, (SUBLANES, LANES), 0)

    def gate_chain(tok0):
        acc = acc_scr[...].reshape(st * hk, LANES).astype(BF16)
        sums = lax.dot_general(ones_row, acc, (((1,), (1,)), ((), ())),
                               preferred_element_type=F32)
        a = jnp.zeros((st, hk), F32)
        for j in range(st):
            a = jnp.where(sub == j, sums[:, j * hk:(j + 1) * hk], a)
        act = 0.5 * a * (1.0 + lax.erf(a * (2.0 ** -0.5)))
        w = act * g_ring[cur, pl.ds(tok0, st), :]
        diag = jnp.concatenate(
            [jnp.where(eye, jnp.broadcast_to(w[j:j + 1, :], (hk, hk)), 0.0).astype(BF16)
             for j in range(st)], axis=0)
        wb = jnp.dot(diag, ones_sq, preferred_element_type=F32)
        wb_scr[...] = wb.reshape(st, hk, LANES)

    @pl.when(step == 0)
    def _():
        route_block(h_ref, 0)
        route_block(h1_ref, 1)
        for a in range(ahead_sets):
            def first(j, c, a=a):
                for kg in range(n_kg):
                    issue_group(a * st + j - PEER_AHEAD, a, j, kg)
                return c
            lax.fori_loop(0, st, first, 0)

    @pl.when(step > 0)
    def _():
        ring_copy((step + 1) % PEER_RING).wait()

    route_source(h2_ref)
    heads_per_body = PEER_HEADS // (tb // (PEER_ALLOCS * st))

    body_tokens = PEER_ALLOCS * st

    def body(it, carry):
        base = pl.multiple_of(it * body_tokens, body_tokens)
        tops = None
        for q in range(PEER_ALLOCS):
            buf = bufs[q]
            nxt = (q + ahead_sets) % PEER_ALLOCS
            tok0 = base + q * st
            head = it * heads_per_body + q // 2
            if q % 2 == 0:
                tops = sub_topk(head)
            else:
                combine(head, tops)
            for j in range(st):
                wait(q, j)

            def loop_a(i, c, buf=buf, nxt=nxt, tok0=tok0):
                for j in range(st):
                    issue_group(tok0 + j, nxt, j, i)
                    x_row = h_ref[pl.ds(tok0 + j, 1), :]
                    for g2 in range(2):
                        kg = 2 * i + g2
                        acc = None
                        for c2 in range(n_lt):
                            xc = jnp.broadcast_to(x_row[:, c2 * LANES:(c2 + 1) * LANES],
                                                  (SUBLANES, LANES))
                            term = buf[j, kg, c2, :, :] * xc
                            acc = term if acc is None else acc + term
                        r0 = pl.multiple_of(kg * SUBLANES, SUBLANES)
                        acc_scr[j, pl.ds(r0, SUBLANES), :] = acc
                return c

            lax.fori_loop(0, n_kg // 2, loop_a, 0, unroll=True)

            gate_chain(pl.multiple_of(tok0, st))

            def loop_b(c2, c, buf=buf, nxt=nxt, tok0=tok0):
                for j in range(st):
                    issue_group(tok0 + j, nxt, j, n_kg // 2 + c2)
                tile = jnp.zeros((SUBLANES, LANES), F32)
                for j in range(st):
                    wb = wb_scr[j].reshape(n_kg, SUBLANES, LANES)
                    prod = buf[j, :, n_lt + c2, :, :] * wb
                    row = jnp.sum(jnp.sum(prod, axis=0), axis=0, keepdims=True)
                    tile = jnp.where(sub == j, jnp.broadcast_to(row, (SUBLANES, LANES)), tile)
                col = pl.multiple_of(c2 * LANES, LANES)
                y_scr[pl.ds(pl.multiple_of(tok0, st), st), pl.ds(col, LANES)] = tile
                return c

            lax.fori_loop(0, n_lt, loop_b, 0, unroll=True)
        return carry

    lax.fori_loop(0, tb // body_tokens, body, 0)
    route_publish((step + 2) % PEER_RING)

    @pl.when(step == pl.num_programs(0) - 1)
    def _():
        ring_copy((step + 2) % PEER_RING).wait()
        for t in range(PEER_AHEAD):
            wait((t // st) % PEER_ALLOCS, t % st)

    o_ref[...] = _ln(alpha * h_ref[...] + y_scr[...], lg_ref[...], lb_ref[...])


def _peer(h1, wq_bf, keys_bf, uv3, g, b, alpha):
    n, d = h1.shape
    tb = PEER_TOKENS
    hk = PEER_HEADS * PEER_TOPK
    assert SUBLANES == PEER_SET and PEER_AHEAD % PEER_SET == 0
    assert 0 < PEER_AHEAD // PEER_SET < PEER_ALLOCS - 1 and PEER_AHEAD <= tb
    assert hk // SUBLANES == 2 * (d // LANES) and tb == LANES and hk == LANES
    assert tb % (PEER_ALLOCS * PEER_SET) == 0 and n % tb == 0
    assert PEER_ALLOCS * (tb // (PEER_ALLOCS * PEER_SET)) == 2 * PEER_HEADS
    last = n // tb - 1
    fix = lambda i: (0, 0)
    slot_shape = (PEER_SET, hk // SUBLANES, uv3.shape[1], SUBLANES, LANES)
    return pl.pallas_call(
        functools.partial(_peer_kernel, alpha),
        grid=(n // tb,),
        in_specs=[
            pl.BlockSpec((tb, d), lambda i: (i, 0)),
            pl.BlockSpec((tb, d), lambda i: (jnp.minimum(i + 1, last), 0)),
            pl.BlockSpec((tb, d), lambda i: (jnp.minimum(i + 2, last), 0)),
            pl.BlockSpec(wq_bf.shape, fix),
            pl.BlockSpec(keys_bf.shape, lambda i: (0, 0, 0, 0)),
            pl.BlockSpec(memory_space=pl.ANY),
            pl.BlockSpec((1, d), fix),
            pl.BlockSpec((1, d), fix),
        ],
        out_specs=pl.BlockSpec((tb, d), lambda i: (i, 0)),
        out_shape=jax.ShapeDtypeStruct((n, d), F32),
        scratch_shapes=(
            [pltpu.VMEM(slot_shape, F32) for _ in range(PEER_ALLOCS)]
            + [pltpu.SemaphoreType.DMA((PEER_ALLOCS, PEER_SET)),
               pltpu.VMEM((tb, d), F32),
               pltpu.VMEM((PEER_SET, hk, LANES), F32),
               pltpu.VMEM((PEER_SET, hk, LANES), F32),
               pltpu.VMEM((tb, d), BF16),
               pltpu.VMEM((hk, tb), F32),
               pltpu.VMEM((hk, tb), I32),
               pltpu.VMEM((PEER_RING, tb, hk), F32),
               pltpu.VMEM((tb, hk), I32),
               pltpu.SMEM((PEER_RING, tb, hk), I32),
               pltpu.SemaphoreType.DMA((1,))]),
        compiler_params=pltpu.CompilerParams(
            dimension_semantics=("arbitrary",), vmem_limit_bytes=VMEM_LIMIT),
        name="peer",
    )(h1, h1, h1, wq_bf, keys_bf, uv3, g.reshape(1, d), b.reshape(1, d))


def kernel(x, w_in, b_in, conv_w, conv_b, conv_ln_g, conv_ln_b, w_out, b_out, ln1_g, ln1_b,
           peer_wq, peer_keys, peer_u, peer_v, ln2_g, ln2_b):
    batch, seq, d = x.shape
    depth = w_in.shape[0]
    alpha = (2.0 * depth) ** 0.25
    slopes = 2.0 ** (-8.0 * jnp.arange(1, N_HEADS + 1, dtype=F32) / N_HEADS)
    h = x.reshape(batch * seq, d)
    for l in range(depth):
        u, qkv = _in_proj(h, w_in[l].astype(BF16), b_in[l])
        yc = _conv(u, conv_w[l], conv_b[l], conv_ln_g[l], conv_ln_b[l], batch, seq)
        ya = _attention(qkv, slopes, batch, seq)
        h1 = _out_proj(yc, ya, h, w_out[l].astype(BF16), b_out[l], ln1_g[l], ln1_b[l], alpha)
        uv3 = jnp.concatenate([peer_u[l], peer_v[l]], axis=1).reshape(-1, 2 * d // LANES, LANES)
        h = _peer(h1, peer_wq[l].astype(BF16), peer_keys[l].astype(BF16), uv3,
                  ln2_g[l], ln2_b[l], alpha)
    return h.reshape(batch, seq, d)
```

```python
import functools
import math

import jax
import jax.numpy as jnp
from jax import lax
from jax.experimental import pallas as pl
from jax.experimental.pallas import tpu as pltpu

F32 = jnp.float32
BF16 = jnp.bfloat16
I32 = jnp.int32

D_MODEL = 1024
D_CONV = D_MODEL // 2
D_ATTN = D_MODEL - D_CONV
HEAD_DIM = 64
N_HEADS = D_ATTN // HEAD_DIM
CONV_WIDTH = 31
DILATED_PATTERNS = ((128, 1), (512, 4), (2048, 16))
ATTN_BLOCK = 128
ATTN_UNROLL = 4
PEER_HEADS = 8
PEER_NKEYS = 128
PEER_QDIM = 256
PEER_TOPK = 16
LN_EPS = 1e-5
NEG_INF = -1e30

LANES = 128
SUBLANES = 8
VMEM_LIMIT = 56 * 1024 * 1024

PEER_SET = 8
PEER_ALLOCS = 4
PEER_AHEAD = 2 * PEER_SET
PEER_TOKENS = 256
PEER_RING = 3


def _ln(x, g, b):
    mu = jnp.mean(x, axis=-1, keepdims=True)
    xc = x - mu
    var = jnp.mean(xc * xc, axis=-1, keepdims=True)
    return xc * lax.rsqrt(var + LN_EPS) * g + b


def _in_proj_kernel(x_ref, w_ref, b_ref, u_ref, qkv_ref):
    x = x_ref[...].astype(BF16)
    nu = u_ref.shape[1]
    u_ref[...] = jnp.dot(x, w_ref[:, :nu], preferred_element_type=F32) + b_ref[:, :nu]
    qkv_ref[...] = jnp.dot(x, w_ref[:, nu:], preferred_element_type=F32) + b_ref[:, nu:]


def _in_proj(x2, w_bf, b, tm=256):
    n, d = x2.shape
    dout = w_bf.shape[1]
    nu = 2 * D_CONV
    return pl.pallas_call(
        _in_proj_kernel,
        grid=(n // tm,),
        in_specs=[
            pl.BlockSpec((tm, d), lambda i: (i, 0)),
            pl.BlockSpec((d, dout), lambda i: (0, 0)),
            pl.BlockSpec((1, dout), lambda i: (0, 0)),
        ],
        out_specs=[
            pl.BlockSpec((tm, nu), lambda i: (i, 0)),
            pl.BlockSpec((tm, dout - nu), lambda i: (i, 0)),
        ],
        out_shape=[
            jax.ShapeDtypeStruct((n, nu), F32),
            jax.ShapeDtypeStruct((n, dout - nu), F32),
        ],
        compiler_params=pltpu.CompilerParams(
            dimension_semantics=("arbitrary",), vmem_limit_bytes=VMEM_LIMIT),
        name="in_proj",
    )(x2, w_bf, b.reshape(1, dout))


CONV_ROWS = 64
CONV_PAD = 32


def _conv_kernel(u_ref, w_ref, cb_ref, g_ref, b_ref, o_ref, pad_ref):
    s = o_ref.shape[0]
    pad_ref[0:CONV_PAD, :] = jnp.zeros((CONV_PAD, D_CONV), F32)

    def glu(i, c):
        r0 = pl.multiple_of(i * 256, 256)
        a = u_ref[pl.ds(r0, 256), 0:D_CONV]
        gate = u_ref[pl.ds(r0, 256), D_CONV:2 * D_CONV]
        pad_ref[pl.ds(CONV_PAD + r0, 256), :] = a * jax.nn.sigmoid(gate)
        return c

    lax.fori_loop(0, s // 256, glu, 0)

    off = CONV_PAD - (CONV_WIDTH - 1)

    def tile(i, c):
        r0 = pl.multiple_of(i * CONV_ROWS, CONV_ROWS)
        acc = jnp.broadcast_to(cb_ref[...], (CONV_ROWS, D_CONV))
        win = pad_ref[pl.ds(r0, CONV_ROWS + CONV_PAD), :]
        for j in range(CONV_WIDTH):
            acc = acc + w_ref[j:j + 1, :] * win[off + j:off + j + CONV_ROWS, :]
        y = _ln(acc, g_ref[...], b_ref[...])
        o_ref[pl.ds(r0, CONV_ROWS), :] = (y * jax.nn.sigmoid(y)).astype(o_ref.dtype)
        return c

    lax.fori_loop(0, s // CONV_ROWS, tile, 0)


def _conv(u, conv_w, conv_b, g, b, batch, seq):
    n = u.shape[0]
    return pl.pallas_call(
        _conv_kernel,
        grid=(batch,),
        in_specs=[
            pl.BlockSpec((seq, 2 * D_CONV), lambda i: (i, 0)),
            pl.BlockSpec((CONV_WIDTH, D_CONV), lambda i: (0, 0)),
            pl.BlockSpec((1, D_CONV), lambda i: (0, 0)),
            pl.BlockSpec((1, D_CONV), lambda i: (0, 0)),
            pl.BlockSpec((1, D_CONV), lambda i: (0, 0)),
        ],
        out_specs=pl.BlockSpec((seq, D_CONV), lambda i: (i, 0)),
        out_shape=jax.ShapeDtypeStruct((n, D_CONV), BF16),
        scratch_shapes=[pltpu.VMEM((seq + CONV_PAD, D_CONV), F32)],
        compiler_params=pltpu.CompilerParams(
            dimension_semantics=("arbitrary",), vmem_limit_bytes=VMEM_LIMIT),
        name="conv",
    )(u, conv_w, conv_b.reshape(1, -1), g.reshape(1, -1), b.reshape(1, -1))


def _attn_kernel(slopes_ref, q_ref, k_ref, v_ref, o_ref, bias_ref, oacc_ref, m_ref, l_ref):
    blk = ATTN_BLOCK
    s_len = q_ref.shape[0]
    hp = pl.program_id(1)
    n_pat = len(DILATED_PATTERNS)

    lane = lax.broadcasted_iota(I32, (blk, LANES), 1)
    head0 = lane < HEAD_DIM
    lane2 = lax.broadcasted_iota(I32, (2 * blk, LANES), 1)
    head0_2 = lane2 < HEAD_DIM

    qi = lax.broadcasted_iota(I32, (blk, 2 * blk), 0)
    kj = lax.broadcasted_iota(I32, (blk, 2 * blk), 1)
    dist = qi + blk - kj
    is_prev = (kj < blk).astype(F32)
    for p, (window, dil) in enumerate(DILATED_PATTERNS):
        n_back = window // dil
        valid = (dist >= 0) & (dist <= n_back)
        dpos = (dist * dil).astype(F32)
        for hh in range(2):
            slope = slopes_ref[2 * hp + hh]
            bias_ref[p, hh] = jnp.where(valid, -slope * dpos, NEG_INF)

    def units(p, dil, specs):
        def rows(ref, start):
            if dil == 1:
                return ref[pl.ds(start, blk), :]
            return ref[pl.ds(start, blk, stride=dil), :]

        scores = []
        values = []
        for cur, prev, first in specs:
            qb = rows(q_ref, cur) * (HEAD_DIM ** -0.5)
            if prev is None:
                kc = rows(k_ref, cur)
                vc = rows(v_ref, cur)
                hmask = head0
            else:
                kc = jnp.concatenate([rows(k_ref, prev), rows(k_ref, cur)], axis=0)
                vc = jnp.concatenate([rows(v_ref, prev), rows(v_ref, cur)], axis=0)
                hmask = head0_2
                pen = jnp.where(first, NEG_INF, 0.0) * is_prev
            kb = kc.astype(BF16)
            for hh in range(2):
                sel = head0 if hh == 0 else jnp.logical_not(head0)
                vsel = hmask if hh == 0 else jnp.logical_not(hmask)
                qh = jnp.where(sel, qb, 0.0).astype(BF16)
                values.append(jnp.where(vsel, vc, 0.0).astype(BF16))
                sc = lax.dot_general(qh, kb, (((1,), (1,)), ((), ())),
                                     preferred_element_type=F32)
                if prev is None:
                    sc = sc + bias_ref[p, hh, :, blk:2 * blk]
                else:
                    sc = sc + bias_ref[p, hh] + pen
                scores.append(sc)
        probs = []
        stats = []
        for sc in scores:
            m = jnp.max(sc, axis=1, keepdims=True)
            e = jnp.exp(sc - m)
            stats.append((m, jnp.sum(e, axis=1, keepdims=True)))
            probs.append(e.astype(BF16))
        outs = [jnp.dot(e, vh, preferred_element_type=F32) for e, vh in zip(probs, values)]
        for n, (cur, prev, first) in enumerate(specs):
            (m0, l0), (m1, l1) = stats[2 * n], stats[2 * n + 1]
            if dil == 1:
                idx = pl.ds(cur, blk)
            else:
                idx = pl.ds(cur, blk, stride=dil)
            oacc_ref[p, idx, :] = outs[2 * n] + outs[2 * n + 1]
            m_ref[p, idx, :] = jnp.where(head0, jnp.broadcast_to(m0, (blk, LANES)),
                                         jnp.broadcast_to(m1, (blk, LANES)))
            l_ref[p, idx, :] = jnp.where(head0, jnp.broadcast_to(l0, (blk, LANES)),
                                         jnp.broadcast_to(l1, (blk, LANES)))

    for p, (window, dil) in enumerate(DILATED_PATTERNS):
        stream_len = s_len // dil
        nb = stream_len // blk

        if nb == 1:
            def body(r4, c, p=p, dil=dil):
                units(p, dil, [(r4 * ATTN_UNROLL + uu, None, None) for uu in range(ATTN_UNROLL)])
                return c
            lax.fori_loop(0, dil // ATTN_UNROLL, body, 0)
        else:
            def body(u4, c, p=p, dil=dil, nb=nb):
                specs = []
                for uu in range(ATTN_UNROLL):
                    u = u4 * ATTN_UNROLL + uu
                    r = u // nb
                    i = u - r * nb
                    cur = r + i * (blk * dil)
                    prev = r + jnp.maximum(i - 1, 0) * (blk * dil)
                    if dil == 1:
                        cur = pl.multiple_of(cur, blk)
                        prev = pl.multiple_of(prev, blk)
                    specs.append((cur, prev, i == 0))
                units(p, dil, specs)
                return c
            lax.fori_loop(0, dil * nb // ATTN_UNROLL, body, 0)

    def merge(i, c):
        r0 = pl.multiple_of(i * 256, 256)
        ms = [m_ref[p, pl.ds(r0, 256), :] for p in range(n_pat)]
        mx = functools.reduce(jnp.maximum, ms)
        num = None
        den = None
        for p in range(n_pat):
            w = jnp.exp(ms[p] - mx)
            tn = w * oacc_ref[p, pl.ds(r0, 256), :]
            td = w * l_ref[p, pl.ds(r0, 256), :]
            num = tn if num is None else num + tn
            den = td if den is None else den + td
        o_ref[pl.ds(r0, 256), :] = (num / den).astype(o_ref.dtype)
        return c

    lax.fori_loop(0, s_len // 256, merge, 0)


def _attention(qkv, slopes, batch, seq):
    n = qkv.shape[0]
    n_hp = D_ATTN // LANES
    n_pat = len(DILATED_PATTERNS)
    return pl.pallas_call(
        _attn_kernel,
        grid=(batch, n_hp),
        in_specs=[
            pl.BlockSpec(memory_space=pltpu.SMEM),
            pl.BlockSpec((seq, LANES), lambda b, h: (b, h)),
            pl.BlockSpec((seq, LANES), lambda b, h: (b, n_hp + h)),
            pl.BlockSpec((seq, LANES), lambda b, h: (b, 2 * n_hp + h)),
        ],
        out_specs=pl.BlockSpec((seq, LANES), lambda b, h: (b, h)),
        out_shape=jax.ShapeDtypeStruct((n, D_ATTN), BF16),
        scratch_shapes=[
            pltpu.VMEM((n_pat, 2, ATTN_BLOCK, 2 * ATTN_BLOCK), F32),
            pltpu.VMEM((n_pat, seq, LANES), F32),
            pltpu.VMEM((n_pat, seq, LANES), F32),
            pltpu.VMEM((n_pat, seq, LANES), F32),
        ],
        compiler_params=pltpu.CompilerParams(
            dimension_semantics=("arbitrary", "arbitrary"), vmem_limit_bytes=VMEM_LIMIT),
        name="attn",
    )(slopes, qkv, qkv, qkv)


def _out_proj_kernel(alpha, yc_ref, ya_ref, x_ref, wc_ref, wa_ref, b_ref, g_ref, be_ref, o_ref):
    y = jnp.dot(yc_ref[...], wc_ref[...], preferred_element_type=F32)
    y = y + jnp.dot(ya_ref[...], wa_ref[...], preferred_element_type=F32)
    y = y + b_ref[...]
    o_ref[...] = _ln(alpha * x_ref[...] + y, g_ref[...], be_ref[...])


def _out_proj(yc, ya, x2, w_out_bf, b_out, g, b, alpha, tm=256):
    n, d = x2.shape
    row = lambda i: (i, 0)
    fix = lambda i: (0, 0)
    return pl.pallas_call(
        functools.partial(_out_proj_kernel, alpha),
        grid=(n // tm,),
        in_specs=[
            pl.BlockSpec((tm, D_CONV), row),
            pl.BlockSpec((tm, D_ATTN), row),
            pl.BlockSpec((tm, d), row),
            pl.BlockSpec((D_CONV, d), fix),
            pl.BlockSpec((D_ATTN, d), fix),
            pl.BlockSpec((1, d), fix),
            pl.BlockSpec((1, d), fix),
            pl.BlockSpec((1, d), fix),
        ],
        out_specs=pl.BlockSpec((tm, d), row),
        out_shape=jax.ShapeDtypeStruct((n, d), F32),
        compiler_params=pltpu.CompilerParams(
            dimension_semantics=("arbitrary",), vmem_limit_bytes=VMEM_LIMIT),
        name="out_proj",
    )(yc, ya, x2, w_out_bf[:D_CONV], w_out_bf[D_CONV:], b_out.reshape(1, d),
      g.reshape(1, d), b.reshape(1, d))


def _route_fns(q_scr, keys_ref, gt_scr, it_scr):
    tiles = q_scr.shape[0] // LANES

    def where_of(item):
        h = item // tiles
        return h, pl.multiple_of((item - h * tiles) * LANES, LANES)

    k_top = PEER_TOPK
    nk = PEER_NKEYS
    lt = LANES
    iota_n = lax.broadcasted_iota(I32, (nk, lt), 0)
    iota_k = lax.broadcasted_iota(I32, (k_top, lt), 0)
    iota_8 = lax.broadcasted_iota(I32, (SUBLANES, lt), 0)
    neg = -jnp.inf

    half = k_top // 2
    blocks = ([(0, 0), (0, half)] + [(i, 0) for i in range(1, half)])
    pos = jnp.concatenate(
        [iota_8 + (i * k_top + j0) for (i, j0) in blocks] + [(iota_8 + half) * k_top], axis=0)

    def sub_topk(item):
        h, lane0 = where_of(item)
        top_s = []
        top_i = []
        for p in range(2):
            col = pl.multiple_of(h * PEER_QDIM + p * (PEER_QDIM // 2), PEER_QDIM // 2)
            q = q_scr[pl.ds(lane0, lt), pl.ds(col, PEER_QDIM // 2)].astype(BF16)
            s = lax.dot_general(keys_ref[h, p], q, (((1,), (1,)), ((), ())),
                                preferred_element_type=F32)
            ts = jnp.zeros((k_top, lt), F32)
            ti = jnp.zeros((k_top, lt), I32)
            for k in range(k_top):
                m = jnp.max(s, axis=0, keepdims=True)
                am = jnp.min(jnp.where(s == m, iota_n, nk), axis=0, keepdims=True)
                s = jnp.where(iota_n == am, neg, s)
                ts = jnp.where(iota_k == k, m, ts)
                ti = jnp.where(iota_k == k, am, ti)
            top_s.append(ts)
            top_i.append(ti)
        return top_s[0], top_s[1], top_i[0], top_i[1]

    def combine(item, tops):
        h, lane0 = where_of(item)
        sa, sb, ia, ib = tops
        cand = jnp.concatenate(
            [sa[i:i + 1, :] + sb[j0:j0 + SUBLANES, :] for (i, j0) in blocks]
            + [sa[half:, :] + sb[0:1, :]], axis=0)
        eid = jnp.concatenate(
            [ia[i:i + 1, :] * nk + ib[j0:j0 + SUBLANES, :] for (i, j0) in blocks]
            + [ia[half:, :] * nk + ib[0:1, :]], axis=0)
        bs = jnp.zeros((k_top, lt), F32)
        be = jnp.zeros((k_top, lt), I32)
        for k in range(k_top):
            m = jnp.max(cand, axis=0, keepdims=True)
            am = jnp.min(jnp.where(cand == m, pos, k_top * k_top), axis=0, keepdims=True)
            sel = pos == am
            e = jnp.max(jnp.where(sel, eid, -1), axis=0, keepdims=True)
            cand = jnp.where(sel, neg, cand)
            bs = jnp.where(iota_k == k, m, bs)
            be = jnp.where(iota_k == k, e, be)
        ex = jnp.exp(bs - bs[0:1, :])
        gate = ex / jnp.sum(ex, axis=0, keepdims=True)
        r0 = pl.multiple_of(h * k_top, k_top)
        gt_scr[pl.ds(r0, k_top), pl.ds(lane0, lt)] = gate
        it_scr[pl.ds(r0, k_top), pl.ds(lane0, lt)] = be

    return sub_topk, combine


def _peer_kernel(alpha, h_ref, h1_ref, h2_ref, wq_ref, keys_ref, uv_ref, lg_ref, lb_ref, o_ref,
                 *scratch):
    bufs = scratch[:PEER_ALLOCS]
    (sem, y_scr, acc_scr, wb_scr, q_scr, gt_scr, it_scr, g_ring, idx_stage, idx_ring,
     ring_sem) = scratch[PEER_ALLOCS:]
    tb, d = h_ref.shape
    hk = PEER_HEADS * PEER_TOPK
    n_lt = d // LANES
    n_kg = hk // SUBLANES
    st = PEER_SET
    ahead_sets = PEER_AHEAD // st
    step = pl.program_id(0)
    cur = step % PEER_RING
    sub_topk, combine = _route_fns(q_scr, keys_ref, gt_scr, it_scr)
    n_items = PEER_HEADS * (tb // LANES)

    def route_source(src_ref):
        q_scr[...] = jnp.dot(src_ref[...].astype(BF16), wq_ref[...], preferred_element_type=F32)

    def ring_copy(slot):
        return pltpu.make_async_copy(idx_stage, idx_ring.at[slot], ring_sem.at[0])

    def route_publish(slot):
        g_ring[slot] = gt_scr[...].T
        idx_stage[...] = it_scr[...].T
        ring_copy(slot).start()

    def route_block(src_ref, slot):
        route_source(src_ref)

        def one(item, c):
            combine(item, sub_topk(item))
            return c
        lax.fori_loop(0, n_items, one, 0)
        route_publish(slot)
        ring_copy(slot).wait()

    def issue_group(tok, alloc, slot, kg):
        la = tok + PEER_AHEAD
        over = la >= tb
        ring = jnp.where(over, (step + 1) % PEER_RING, cur)
        row = jnp.where(over, la - tb, la)
        for r in range(SUBLANES):
            e = idx_ring[ring, row, kg * SUBLANES + r]
            pltpu.make_async_copy(
                uv_ref.at[e], bufs[alloc].at[slot, kg, :, r, :],
                sem.at[alloc, slot]).start(priority=r % 2)

    def wait(alloc, slot):
        pltpu.make_async_copy(
            bufs[alloc].at[slot], bufs[alloc].at[slot], sem.at[alloc, slot]).wait()

    eye = (lax.broadcasted_iota(I32, (hk, hk), 0) == lax.broadcasted_iota(I32, (hk, hk), 1))
    ones_row = jnp.ones((SUBLANES, LANES), BF16)
    ones_sq = jnp.ones((hk, LANES), BF16)
    sub = lax.broadcasted_iota(I32, (SUBLANES, LANES), 0)

    def gate_chain(tok0):
        acc = acc_scr[...].reshape(st * hk, LANES).astype(BF16)
        sums = lax.dot_general(ones_row, acc, (((1,), (1,)), ((), ())),
                               preferred_element_type=F32)
        a = jnp.zeros((st, hk), F32)
        for j in range(st):
            a = jnp.where(sub == j, sums[:, j * hk:(j + 1) * hk], a)
        act = 0.5 * a * (1.0 + lax.erf(a * (2.0 ** -0.5)))
        w = act * g_ring[cur, pl.ds(tok0, st), :]
        diag = jnp.concatenate(
            [jnp.where(eye, jnp.broadcast_to(w[j:j + 1, :], (hk, hk)), 0.0).astype(BF16)
             for j in range(st)], axis=0)
        wb = jnp.dot(diag, ones_sq, preferred_element_type=F32)
        wb_scr[...] = wb.reshape(st, hk, LANES)

    @pl.when(step == 0)
    def _():
        route_block(h_ref, 0)
        route_block(h1_ref, 1)
        for a in range(ahead_sets):
            def first(j, c, a=a):
                for kg in range(n_kg):
                    issue_group(a * st + j - PEER_AHEAD, a, j, kg)
                return c
            lax.fori_loop(0, st, first, 0)

    @pl.when(step > 0)
    def _():
        ring_copy((step + 1) % PEER_RING).wait()

    route_source(h2_ref)
    body_tokens = PEER_ALLOCS * st
    items_per_body = n_items // (tb // body_tokens)

    def body(it, carry):
        base = pl.multiple_of(it * body_tokens, body_tokens)
        tops = None
        for q in range(PEER_ALLOCS):
            buf = bufs[q]
            nxt = (q + ahead_sets) % PEER_ALLOCS
            tok0 = base + q * st
            item = it * items_per_body + q // 2
            if q % 2 == 0:
                tops = sub_topk(item)
            else:
                combine(item, tops)
            for j in range(st):
                wait(q, j)

            def loop_a(i, c, buf=buf, nxt=nxt, tok0=tok0):
                for j in range(st):
                    issue_group(tok0 + j, nxt, j, i)
                    x_row = h_ref[pl.ds(tok0 + j, 1), :]
                    for g2 in range(2):
                        kg = 2 * i + g2
                        acc = None
                        for c2 in range(n_lt):
                            xc = jnp.broadcast_to(x_row[:, c2 * LANES:(c2 + 1) * LANES],
                                                  (SUBLANES, LANES))
                            term = buf[j, kg, c2, :, :] * xc
                            acc = term if acc is None else acc + term
                        r0 = pl.multiple_of(kg * SUBLANES, SUBLANES)
                        acc_scr[j, pl.ds(r0, SUBLANES), :] = acc
                return c

            lax.fori_loop(0, n_kg // 2, loop_a, 0, unroll=True)

            gate_chain(pl.multiple_of(tok0, st))

            def loop_b(c2, c, buf=buf, nxt=nxt, tok0=tok0):
                for j in range(st):
                    issue_group(tok0 + j, nxt, j, n_kg // 2 + c2)
                tile = jnp.zeros((SUBLANES, LANES), F32)
                for j in range(st):
                    wb = wb_scr[j].reshape(n_kg, SUBLANES, LANES)
                    prod = buf[j, :, n_lt + c2, :, :] * wb
                    row = jnp.sum(jnp.sum(prod, axis=0), axis=0, keepdims=True)
                    tile = jnp.where(sub == j, jnp.broadcast_to(row, (SUBLANES, LANES)), tile)
                col = pl.multiple_of(c2 * LANES, LANES)
                y_scr[pl.ds(pl.multiple_of(tok0, st), st), pl.ds(col, LANES)] = tile
                return c

            lax.fori_loop(0, n_lt, loop_b, 0, unroll=True)
        return carry

    lax.fori_loop(0, tb // body_tokens, body, 0)
    route_publish((step + 2) % PEER_RING)

    @pl.when(step == pl.num_programs(0) - 1)
    def _():
        ring_copy((step + 2) % PEER_RING).wait()
        for t in range(PEER_AHEAD):
            wait((t // st) % PEER_ALLOCS, t % st)

    o_ref[...] = _ln(alpha * h_ref[...] + y_scr[...], lg_ref[...], lb_ref[...])


def _peer(h1, wq_bf, keys_bf, uv3, g, b, alpha):
    n, d = h1.shape
    tb = PEER_TOKENS
    hk = PEER_HEADS * PEER_TOPK
    assert SUBLANES == PEER_SET and PEER_AHEAD % PEER_SET == 0
    assert 0 < PEER_AHEAD // PEER_SET < PEER_ALLOCS - 1 and PEER_AHEAD <= tb
    assert hk // SUBLANES == 2 * (d // LANES) and tb % LANES == 0 and hk == LANES
    assert tb % (PEER_ALLOCS * PEER_SET) == 0 and n % tb == 0
    assert PEER_ALLOCS * (tb // (PEER_ALLOCS * PEER_SET)) == 2 * PEER_HEADS * (tb // LANES)
    last = n // tb - 1
    fix = lambda i: (0, 0)
    slot_shape = (PEER_SET, hk // SUBLANES, uv3.shape[1], SUBLANES, LANES)
    return pl.pallas_call(
        functools.partial(_peer_kernel, alpha),
        grid=(n // tb,),
        in_specs=[
            pl.BlockSpec((tb, d), lambda i: (i, 0)),
            pl.BlockSpec((tb, d), lambda i: (jnp.minimum(i + 1, last), 0)),
            pl.BlockSpec((tb, d), lambda i: (jnp.minimum(i + 2, last), 0)),
            pl.BlockSpec(wq_bf.shape, fix),
            pl.BlockSpec(keys_bf.shape, lambda i: (0, 0, 0, 0)),
            pl.BlockSpec(memory_space=pl.ANY),
            pl.BlockSpec((1, d), fix),
            pl.BlockSpec((1, d), fix),
        ],
        out_specs=pl.BlockSpec((tb, d), lambda i: (i, 0)),
        out_shape=jax.ShapeDtypeStruct((n, d), F32),
        scratch_shapes=(
            [pltpu.VMEM(slot_shape, F32) for _ in range(PEER_ALLOCS)]
            + [pltpu.SemaphoreType.DMA((PEER_ALLOCS, PEER_SET)),
               pltpu.VMEM((tb, d), F32),
               pltpu.VMEM((PEER_SET, hk, LANES), F32),
               pltpu.VMEM((PEER_SET, hk, LANES), F32),
               pltpu.VMEM((tb, PEER_HEADS * PEER_QDIM), F32),
               pltpu.VMEM((hk, tb), F32),
               pltpu.VMEM((hk, tb), I32),
               pltpu.VMEM((PEER_RING, tb, hk), F32),
               pltpu.VMEM((tb, hk), I32),
               pltpu.SMEM((PEER_RING, tb, hk), I32),
               pltpu.SemaphoreType.DMA((1,))]),
        compiler_params=pltpu.CompilerParams(
            dimension_semantics=("arbitrary",), vmem_limit_bytes=VMEM_LIMIT),
        name="peer",
    )(h1, h1, h1, wq_bf, keys_bf, uv3, g.reshape(1, d), b.reshape(1, d))


def kernel(x, w_in, b_in, conv_w, conv_b, conv_ln_g, conv_ln_b, w_out, b_out, ln1_g, ln1_b,
           peer_wq, peer_keys, peer_u, peer_v, ln2_g, ln2_b):
    batch, seq, d = x.shape
    depth = w_in.shape[0]
    alpha = (2.0 * depth) ** 0.25
    slopes = 2.0 ** (-8.0 * jnp.arange(1, N_HEADS + 1, dtype=F32) / N_HEADS)
    h = x.reshape(batch * seq, d)
    for l in range(depth):
        u, qkv = _in_proj(h, w_in[l].astype(BF16), b_in[l])
        yc = _conv(u, conv_w[l], conv_b[l], conv_ln_g[l], conv_ln_b[l], batch, seq)
        ya = _attention(qkv, slopes, batch, seq)
        h1 = _out_proj(yc, ya, h, w_out[l].astype(BF16), b_out[l], ln1_g[l], ln1_b[l], alpha)
        uv3 = jnp.concatenate([peer_u[l], peer_v[l]], axis=1).reshape(-1, 2 * d // LANES, LANES)
        h = _peer(h1, peer_wq[l].astype(BF16), peer_keys[l].astype(BF16), uv3,
                  ln2_g[l], ln2_b[l], alpha)
    return h.reshape(batch, seq, d)
```

```python
import functools
import math

import jax
import jax.numpy as jnp
from jax import lax
from jax.experimental import pallas as pl
from jax.experimental.pallas import tpu as pltpu

F32 = jnp.float32
BF16 = jnp.bfloat16
I32 = jnp.int32

D_MODEL = 1024
D_CONV = D_MODEL // 2
D_ATTN = D_MODEL - D_CONV
HEAD_DIM = 64
N_HEADS = D_ATTN // HEAD_DIM
CONV_WIDTH = 31
DILATED_PATTERNS = ((128, 1), (512, 4), (2048, 16))
ATTN_BLOCK = 128
ATTN_UNROLL = 4
PEER_HEADS = 8
PEER_NKEYS = 128
PEER_QDIM = 256
PEER_TOPK = 16
LN_EPS = 1e-5
NEG_INF = -1e30

LANES = 128
SUBLANES = 8
VMEM_LIMIT = 56 * 1024 * 1024

PEER_SET = 8
PEER_ALLOCS = 4
PEER_AHEAD = 2 * PEER_SET
PEER_TOKENS = 256
PEER_RING = 3


def _ln(x, g, b):
    mu = jnp.mean(x, axis=-1, keepdims=True)
    xc = x - mu
    var = jnp.mean(xc * xc, axis=-1, keepdims=True)
    return xc * lax.rsqrt(var + LN_EPS) * g + b


def _in_proj_kernel(x_ref, w_ref, b_ref, u_ref, qkv_ref):
    x = x_ref[...].astype(BF16)
    nu = u_ref.shape[1]
    u_ref[...] = jnp.dot(x, w_ref[:, :nu], preferred_element_type=F32) + b_ref[:, :nu]
    qkv_ref[...] = jnp.dot(x, w_ref[:, nu:], preferred_element_type=F32) + b_ref[:, nu:]


def _in_proj(x2, w_bf, b, tm=256):
    n, d = x2.shape
    dout = w_bf.shape[1]
    nu = 2 * D_CONV
    return pl.pallas_call(
        _in_proj_kernel,
        grid=(n // tm,),
        in_specs=[
            pl.BlockSpec((tm, d), lambda i: (i, 0)),
            pl.BlockSpec((d, dout), lambda i: (0, 0)),
            pl.BlockSpec((1, dout), lambda i: (0, 0)),
        ],
        out_specs=[
            pl.BlockSpec((tm, nu), lambda i: (i, 0)),
            pl.BlockSpec((tm, dout - nu), lambda i: (i, 0)),
        ],
        out_shape=[
            jax.ShapeDtypeStruct((n, nu), F32),
            jax.ShapeDtypeStruct((n, dout - nu), F32),
        ],
        compiler_params=pltpu.CompilerParams(
            dimension_semantics=("arbitrary",), vmem_limit_bytes=VMEM_LIMIT),
        name="in_proj",
    )(x2, w_bf, b.reshape(1, dout))


CONV_ROWS = 64
CONV_PAD = 32


def _conv_kernel(u_ref, w_ref, cb_ref, g_ref, b_ref, o_ref, pad_ref):
    s = o_ref.shape[0]
    pad_ref[0:CONV_PAD, :] = jnp.zeros((CONV_PAD, D_CONV), F32)

    def glu(i, c):
        r0 = pl.multiple_of(i * 256, 256)
        a = u_ref[pl.ds(r0, 256), 0:D_CONV]
        gate = u_ref[pl.ds(r0, 256), D_CONV:2 * D_CONV]
        pad_ref[pl.ds(CONV_PAD + r0, 256), :] = a * jax.nn.sigmoid(gate)
        return c

    lax.fori_loop(0, s // 256, glu, 0)

    off = CONV_PAD - (CONV_WIDTH - 1)

    def tile(i, c):
        r0 = pl.multiple_of(i * CONV_ROWS, CONV_ROWS)
        acc = jnp.broadcast_to(cb_ref[...], (CONV_ROWS, D_CONV))
        win = pad_ref[pl.ds(r0, CONV_ROWS + CONV_PAD), :]
        for j in range(CONV_WIDTH):
            acc = acc + w_ref[j:j + 1, :] * win[off + j:off + j + CONV_ROWS, :]
        y = _ln(acc, g_ref[...], b_ref[...])
        o_ref[pl.ds(r0, CONV_ROWS), :] = (y * jax.nn.sigmoid(y)).astype(o_ref.dtype)
        return c

    lax.fori_loop(0, s // CONV_ROWS, tile, 0)


def _conv(u, conv_w, conv_b, g, b, batch, seq):
    n = u.shape[0]
    return pl.pallas_call(
        _conv_kernel,
        grid=(batch,),
        in_specs=[
            pl.BlockSpec((seq, 2 * D_CONV), lambda i: (i, 0)),
            pl.BlockSpec((CONV_WIDTH, D_CONV), lambda i: (0, 0)),
            pl.BlockSpec((1, D_CONV), lambda i: (0, 0)),
            pl.BlockSpec((1, D_CONV), lambda i: (0, 0)),
            pl.BlockSpec((1, D_CONV), lambda i: (0, 0)),
        ],
        out_specs=pl.BlockSpec((seq, D_CONV), lambda i: (i, 0)),
        out_shape=jax.ShapeDtypeStruct((n, D_CONV), BF16),
        scratch_shapes=[pltpu.VMEM((seq + CONV_PAD, D_CONV), F32)],
        compiler_params=pltpu.CompilerParams(
            dimension_semantics=("arbitrary",), vmem_limit_bytes=VMEM_LIMIT),
        name="conv",
    )(u, conv_w, conv_b.reshape(1, -1), g.reshape(1, -1), b.reshape(1, -1))


def _attn_kernel(slopes_ref, q_ref, k_ref, v_ref, o_ref, bias_ref, oacc_ref, m_ref, l_ref):
    blk = ATTN_BLOCK
    s_len = q_ref.shape[0]
    hp = pl.program_id(1)
    n_pat = len(DILATED_PATTERNS)

    lane = lax.broadcasted_iota(I32, (blk, LANES), 1)
    head0 = lane < HEAD_DIM
    lane2 = lax.broadcasted_iota(I32, (2 * blk, LANES), 1)
    head0_2 = lane2 < HEAD_DIM

    qi = lax.broadcasted_iota(I32, (blk, 2 * blk), 0)
    kj = lax.broadcasted_iota(I32, (blk, 2 * blk), 1)
    dist = qi + blk - kj
    is_prev = (kj < blk).astype(F32)
    for p, (window, dil) in enumerate(DILATED_PATTERNS):
        n_back = window // dil
        valid = (dist >= 0) & (dist <= n_back)
        dpos = (dist * dil).astype(F32)
        for hh in range(2):
            slope = slopes_ref[2 * hp + hh]
            bias_ref[p, hh] = jnp.where(valid, -slope * dpos, NEG_INF)

    def units(p, dil, specs):
        def rows(ref, start):
            if dil == 1:
                return ref[pl.ds(start, blk), :]
            return ref[pl.ds(start, blk, stride=dil), :]

        scores = []
        values = []
        for cur, prev, first in specs:
            qb = rows(q_ref, cur) * (HEAD_DIM ** -0.5)
            if prev is None:
                kc = rows(k_ref, cur)
                vc = rows(v_ref, cur)
                hmask = head0
            else:
                kc = jnp.concatenate([rows(k_ref, prev), rows(k_ref, cur)], axis=0)
                vc = jnp.concatenate([rows(v_ref, prev), rows(v_ref, cur)], axis=0)
                hmask = head0_2
                pen = jnp.where(first, NEG_INF, 0.0) * is_prev
            kb = kc.astype(BF16)
            for hh in range(2):
                sel = head0 if hh == 0 else jnp.logical_not(head0)
                vsel = hmask if hh == 0 else jnp.logical_not(hmask)
                qh = jnp.where(sel, qb, 0.0).astype(BF16)
                values.append(jnp.where(vsel, vc, 0.0).astype(BF16))
                sc = lax.dot_general(qh, kb, (((1,), (1,)), ((), ())),
                                     preferred_element_type=F32)
                if prev is None:
                    sc = sc + bias_ref[p, hh, :, blk:2 * blk]
                else:
                    sc = sc + bias_ref[p, hh] + pen
                scores.append(sc)
        probs = []
        stats = []
        for sc in scores:
            m = jnp.max(sc, axis=1, keepdims=True)
            e = jnp.exp(sc - m)
            stats.append((m, jnp.sum(e, axis=1, keepdims=True)))
            probs.append(e.astype(BF16))
        outs = [jnp.dot(e, vh, preferred_element_type=F32) for e, vh in zip(probs, values)]
        for n, (cur, prev, first) in enumerate(specs):
            (m0, l0), (m1, l1) = stats[2 * n], stats[2 * n + 1]
            if dil == 1:
                idx = pl.ds(cur, blk)
            else:
                idx = pl.ds(cur, blk, stride=dil)
            oacc_ref[p, idx, :] = outs[2 * n] + outs[2 * n + 1]
            m_ref[p, idx, :] = jnp.where(head0, jnp.broadcast_to(m0, (blk, LANES)),
                                         jnp.broadcast_to(m1, (blk, LANES)))
            l_ref[p, idx, :] = jnp.where(head0, jnp.broadcast_to(l0, (blk, LANES)),
                                         jnp.broadcast_to(l1, (blk, LANES)))

    for p, (window, dil) in enumerate(DILATED_PATTERNS):
        stream_len = s_len // dil
        nb = stream_len // blk

        if nb == 1:
            def body(r4, c, p=p, dil=dil):
                units(p, dil, [(r4 * ATTN_UNROLL + uu, None, None) for uu in range(ATTN_UNROLL)])
                return c
            lax.fori_loop(0, dil // ATTN_UNROLL, body, 0)
        else:
            def body(u4, c, p=p, dil=dil, nb=nb):
                specs = []
                for uu in range(ATTN_UNROLL):
                    u = u4 * ATTN_UNROLL + uu
                    r = u // nb
                    i = u - r * nb
                    cur = r + i * (blk * dil)
                    prev = r + jnp.maximum(i - 1, 0) * (blk * dil)
                    if dil == 1:
                        cur = pl.multiple_of(cur, blk)
                        prev = pl.multiple_of(prev, blk)
                    specs.append((cur, prev, i == 0))
                units(p, dil, specs)
                return c
            lax.fori_loop(0, dil * nb // ATTN_UNROLL, body, 0)

    def merge(i, c):
        r0 = pl.multiple_of(i * 256, 256)
        ms = [m_ref[p, pl.ds(r0, 256), :] for p in range(n_pat)]
        mx = functools.reduce(jnp.maximum, ms)
        num = None
        den = None
        for p in range(n_pat):
            w = jnp.exp(ms[p] - mx)
            tn = w * oacc_ref[p, pl.ds(r0, 256), :]
            td = w * l_ref[p, pl.ds(r0, 256), :]
            num = tn if num is None else num + tn
            den = td if den is None else den + td
        o_ref[pl.ds(r0, 256), :] = (num / den).astype(o_ref.dtype)
        return c

    lax.fori_loop(0, s_len // 256, merge, 0)


def _attention(qkv, slopes, batch, seq):
    n = qkv.shape[0]
    n_hp = D_ATTN // LANES
    n_pat = len(DILATED_PATTERNS)
    return pl.pallas_call(
        _attn_kernel,
        grid=(batch, n_hp),
        in_specs=[
            pl.BlockSpec(memory_space=pltpu.SMEM),
            pl.BlockSpec((seq, LANES), lambda b, h: (b, h)),
            pl.BlockSpec((seq, LANES), lambda b, h: (b, n_hp + h)),
            pl.BlockSpec((seq, LANES), lambda b, h: (b, 2 * n_hp + h)),
        ],
        out_specs=pl.BlockSpec((seq, LANES), lambda b, h: (b, h)),
        out_shape=jax.ShapeDtypeStruct((n, D_ATTN), BF16),
        scratch_shapes=[
            pltpu.VMEM((n_pat, 2, ATTN_BLOCK, 2 * ATTN_BLOCK), F32),
            pltpu.VMEM((n_pat, seq, LANES), F32),
            pltpu.VMEM((n_pat, seq, LANES), F32),
            pltpu.VMEM((n_pat, seq, LANES), F32),
        ],
        compiler_params=pltpu.CompilerParams(
            dimension_semantics=("arbitrary", "arbitrary"), vmem_limit_bytes=VMEM_LIMIT),
        name="attn",
    )(slopes, qkv, qkv, qkv)


def _out_proj_kernel(alpha, yc_ref, ya_ref, x_ref, wc_ref, wa_ref, b_ref, g_ref, be_ref, o_ref):
    y = jnp.dot(yc_ref[...], wc_ref[...], preferred_element_type=F32)
    y = y + jnp.dot(ya_ref[...], wa_ref[...], preferred_element_type=F32)
    y = y + b_ref[...]
    o_ref[...] = _ln(alpha * x_ref[...] + y, g_ref[...], be_ref[...])


def _out_proj(yc, ya, x2, w_out_bf, b_out, g, b, alpha, tm=256):
    n, d = x2.shape
    row = lambda i: (i, 0)
    fix = lambda i: (0, 0)
    return pl.pallas_call(
        functools.partial(_out_proj_kernel, alpha),
        grid=(n // tm,),
        in_specs=[
            pl.BlockSpec((tm, D_CONV), row),
            pl.BlockSpec((tm, D_ATTN), row),
            pl.BlockSpec((tm, d), row),
            pl.BlockSpec((D_CONV, d), fix),
            pl.BlockSpec((D_ATTN, d), fix),
            pl.BlockSpec((1, d), fix),
            pl.BlockSpec((1, d), fix),
            pl.BlockSpec((1, d), fix),
        ],
        out_specs=pl.BlockSpec((tm, d), row),
        out_shape=jax.ShapeDtypeStruct((n, d), F32),
        compiler_params=pltpu.CompilerParams(
            dimension_semantics=("arbitrary",), vmem_limit_bytes=VMEM_LIMIT),
        name="out_proj",
    )(yc, ya, x2, w_out_bf[:D_CONV], w_out_bf[D_CONV:], b_out.reshape(1, d),
      g.reshape(1, d), b.reshape(1, d))


def _route_fns(q_scr, keys_ref, gt_scr, it_scr):
    tiles = q_scr.shape[0] // LANES

    def where_of(item):
        h = item // tiles
        return h, pl.multiple_of((item - h * tiles) * LANES, LANES)

    k_top = PEER_TOPK
    nk = PEER_NKEYS
    lt = LANES
    iota_n = lax.broadcasted_iota(I32, (nk, lt), 0)
    iota_k = lax.broadcasted_iota(I32, (k_top, lt), 0)
    iota_8 = lax.broadcasted_iota(I32, (SUBLANES, lt), 0)
    neg = -jnp.inf

    half = k_top // 2
    blocks = ([(0, 0), (0, half)] + [(i, 0) for i in range(1, half)])
    pos = jnp.concatenate(
        [iota_8 + (i * k_top + j0) for (i, j0) in blocks] + [(iota_8 + half) * k_top], axis=0)

    def pin(x, mark):
        if mark is None:
            return x
        head = jnp.where(iota_8 == (mark | (1 << 30)), 0.0, x[:SUBLANES, :])
        return jnp.concatenate([head, x[SUBLANES:, :]], axis=0)

    def topk_init(item):
        h, lane0 = where_of(item)
        state = []
        for p in range(2):
            col = pl.multiple_of(h * PEER_QDIM + p * (PEER_QDIM // 2), PEER_QDIM // 2)
            q = q_scr[pl.ds(lane0, lt), pl.ds(col, PEER_QDIM // 2)].astype(BF16)
            s = lax.dot_general(keys_ref[h, p], q, (((1,), (1,)), ((), ())),
                                preferred_element_type=F32)
            state.append((s, jnp.zeros((k_top, lt), F32), jnp.zeros((k_top, lt), I32)))
        return state

    def topk_step(state, k, mark=None):
        out = []
        for s, ts, ti in state:
            s = pin(s, mark)
            m = jnp.max(s, axis=0, keepdims=True)
            am = jnp.min(jnp.where(s == m, iota_n, nk), axis=0, keepdims=True)
            s = jnp.where(iota_n == am, neg, s)
            out.append((s, jnp.where(iota_k == k, m, ts), jnp.where(iota_k == k, am, ti)))
        return out

    def topk_finish(state):
        return state[0][1], state[1][1], state[0][2], state[1][2]

    def comb_init(tops):
        sa, sb, ia, ib = tops
        cand = jnp.concatenate(
            [sa[i:i + 1, :] + sb[j0:j0 + SUBLANES, :] for (i, j0) in blocks]
            + [sa[half:, :] + sb[0:1, :]], axis=0)
        eid = jnp.concatenate(
            [ia[i:i + 1, :] * nk + ib[j0:j0 + SUBLANES, :] for (i, j0) in blocks]
            + [ia[half:, :] * nk + ib[0:1, :]], axis=0)
        return cand, eid, jnp.zeros((k_top, lt), F32), jnp.zeros((k_top, lt), I32)

    def comb_step(state, k, mark=None):
        cand, eid, bs, be = state
        cand = pin(cand, mark)
        m = jnp.max(cand, axis=0, keepdims=True)
        am = jnp.min(jnp.where(cand == m, pos, k_top * k_top), axis=0, keepdims=True)
        sel = pos == am
        e = jnp.max(jnp.where(sel, eid, -1), axis=0, keepdims=True)
        cand = jnp.where(sel, neg, cand)
        return cand, eid, jnp.where(iota_k == k, m, bs), jnp.where(iota_k == k, e, be)

    def comb_finish(item, state):
        h, lane0 = where_of(item)
        _, _, bs, be = state
        ex = jnp.exp(bs - bs[0:1, :])
        gate = ex / jnp.sum(ex, axis=0, keepdims=True)
        r0 = pl.multiple_of(h * k_top, k_top)
        gt_scr[pl.ds(r0, k_top), pl.ds(lane0, lt)] = gate
        it_scr[pl.ds(r0, k_top), pl.ds(lane0, lt)] = be

    return (topk_init, topk_step, topk_finish), (comb_init, comb_step, comb_finish)


def _peer_kernel(alpha, h_ref, h1_ref, h2_ref, wq_ref, keys_ref, uv_ref, lg_ref, lb_ref, o_ref,
                 *scratch):
    bufs = scratch[:PEER_ALLOCS]
    (sem, y_scr, acc_scr, wb_scr, q_scr, gt_scr, it_scr, g_ring, idx_stage, idx_ring,
     ring_sem) = scratch[PEER_ALLOCS:]
    tb, d = h_ref.shape
    hk = PEER_HEADS * PEER_TOPK
    n_lt = d // LANES
    n_kg = hk // SUBLANES
    st = PEER_SET
    ahead_sets = PEER_AHEAD // st
    step = pl.program_id(0)
    cur = step % PEER_RING
    (tk_init, tk_step, tk_finish), (cb_init, cb_step, cb_finish) = _route_fns(
        q_scr, keys_ref, gt_scr, it_scr)
    n_items = PEER_HEADS * (tb // LANES)

    def route_source(src_ref):
        q_scr[...] = jnp.dot(src_ref[...].astype(BF16), wq_ref[...], preferred_element_type=F32)

    def ring_copy(slot):
        return pltpu.make_async_copy(idx_stage, idx_ring.at[slot], ring_sem.at[0])

    def route_publish(slot):
        g_ring[slot] = gt_scr[...].T
        idx_stage[...] = it_scr[...].T
        ring_copy(slot).start()

    def route_block(src_ref, slot):
        route_source(src_ref)

        def one(item, c):
            ts = tk_init(item)
            for k in range(PEER_TOPK):
                ts = tk_step(ts, k)
            cs = cb_init(tk_finish(ts))
            for k in range(PEER_TOPK):
                cs = cb_step(cs, k)
            cb_finish(item, cs)
            return c
        lax.fori_loop(0, n_items, one, 0)
        route_publish(slot)
        ring_copy(slot).wait()

    def issue_group(tok, alloc, slot, kg):
        la = tok + PEER_AHEAD
        over = la >= tb
        ring = jnp.where(over, (step + 1) % PEER_RING, cur)
        row = jnp.where(over, la - tb, la)
        for r in range(SUBLANES):
            e = idx_ring[ring, row, kg * SUBLANES + r]
            pltpu.make_async_copy(
                uv_ref.at[e], bufs[alloc].at[slot, kg, :, r, :],
                sem.at[alloc, slot]).start(priority=r % 2)
        return e

    def wait(alloc, slot):
        pltpu.make_async_copy(
            bufs[alloc].at[slot], bufs[alloc].at[slot], sem.at[alloc, slot]).wait()

    eye = (lax.broadcasted_iota(I32, (hk, hk), 0) == lax.broadcasted_iota(I32, (hk, hk), 1))
    ones_row = jnp.ones((SUBLANES, LANES), BF16)
    ones_sq = jnp.ones((hk, LANES), BF16)
    sub = lax.broadcasted_iota(I32, (SUBLANES, LANES), 0)

    def gate_chain(tok0):
        acc = acc_scr[...].reshape(st * hk, LANES).astype(BF16)
        sums = lax.dot_general(ones_row, acc, (((1,), (1,)), ((), ())),
                               preferred_element_type=F32)
        a = jnp.zeros((st, hk), F32)
        for j in range(st):
            a = jnp.where(sub == j, sums[:, j * hk:(j + 1) * hk], a)
        act = 0.5 * a * (1.0 + lax.erf(a * (2.0 ** -0.5)))
        w = act * g_ring[cur, pl.ds(tok0, st), :]
        diag = jnp.concatenate(
            [jnp.where(eye, jnp.broadcast_to(w[j:j + 1, :], (hk, hk)), 0.0).astype(BF16)
             for j in range(st)], axis=0)
        wb = jnp.dot(diag, ones_sq, preferred_element_type=F32)
        wb_scr[...] = wb.reshape(st, hk, LANES)

    @pl.when(step == 0)
    def _():
        route_block(h_ref, 0)
        route_block(h1_ref, 1)
        for a in range(ahead_sets):
            def first(j, c, a=a):
                for kg in range(n_kg):
                    issue_group(a * st + j - PEER_AHEAD, a, j, kg)
                return c
            lax.fori_loop(0, st, first, 0)

    @pl.when(step > 0)
    def _():
        ring_copy((step + 1) % PEER_RING).wait()

    route_source(h2_ref)
    body_tokens = PEER_ALLOCS * st
    items_per_body = n_items // (tb // body_tokens)

    def body(it, carry):
        base = pl.multiple_of(it * body_tokens, body_tokens)
        tops = None
        for q in range(PEER_ALLOCS):
            buf = bufs[q]
            nxt = (q + ahead_sets) % PEER_ALLOCS
            tok0 = base + q * st
            item = it * items_per_body + q // 2
            route = tk_init(item) if q % 2 == 0 else cb_init(tops)
            route_step = tk_step if q % 2 == 0 else cb_step
            for j in range(st):
                wait(q, j)

            for i in range(n_kg // 2):
                for j in range(st):
                    mark = issue_group(tok0 + j, nxt, j, i)
                    x_row = h_ref[pl.ds(tok0 + j, 1), :]
                    for g2 in range(2):
                        kg = 2 * i + g2
                        acc = None
                        for c2 in range(n_lt):
                            xc = jnp.broadcast_to(x_row[:, c2 * LANES:(c2 + 1) * LANES],
                                                  (SUBLANES, LANES))
                            term = buf[j, kg, c2, :, :] * xc
                            acc = term if acc is None else acc + term
                        acc_scr[j, kg * SUBLANES:(kg + 1) * SUBLANES, :] = acc
                route = route_step(route, i, mark)

            gate_chain(pl.multiple_of(tok0, st))

            for c2 in range(n_lt):
                for j in range(st):
                    mark = issue_group(tok0 + j, nxt, j, n_kg // 2 + c2)
                tile = jnp.zeros((SUBLANES, LANES), F32)
                for j in range(st):
                    wb = wb_scr[j].reshape(n_kg, SUBLANES, LANES)
                    prod = buf[j, :, n_lt + c2, :, :] * wb
                    row = jnp.sum(jnp.sum(prod, axis=0), axis=0, keepdims=True)
                    tile = jnp.where(sub == j, jnp.broadcast_to(row, (SUBLANES, LANES)), tile)
                y_scr[pl.ds(pl.multiple_of(tok0, st), st), c2 * LANES:(c2 + 1) * LANES] = tile
                route = route_step(route, n_kg // 2 + c2, mark)

            if q % 2 == 0:
                tops = tk_finish(route)
            else:
                cb_finish(item, route)
        return carry

    lax.fori_loop(0, tb // body_tokens, body, 0)
    route_publish((step + 2) % PEER_RING)

    @pl.when(step == pl.num_programs(0) - 1)
    def _():
        ring_copy((step + 2) % PEER_RING).wait()
        for t in range(PEER_AHEAD):
            wait((t // st) % PEER_ALLOCS, t % st)

    o_ref[...] = _ln(alpha * h_ref[...] + y_scr[...], lg_ref[...], lb_ref[...])


def _peer(h1, wq_bf, keys_bf, uv3, g, b, alpha):
    n, d = h1.shape
    tb = PEER_TOKENS
    hk = PEER_HEADS * PEER_TOPK
    assert SUBLANES == PEER_SET and PEER_AHEAD % PEER_SET == 0
    assert 0 < PEER_AHEAD // PEER_SET < PEER_ALLOCS - 1 and PEER_AHEAD <= tb
    assert hk // SUBLANES == 2 * (d // LANES) and tb % LANES == 0 and hk == LANES
    assert tb % (PEER_ALLOCS * PEER_SET) == 0 and n % tb == 0
    assert PEER_ALLOCS * (tb // (PEER_ALLOCS * PEER_SET)) == 2 * PEER_HEADS * (tb // LANES)
    last = n // tb - 1
    fix = lambda i: (0, 0)
    slot_shape = (PEER_SET, hk // SUBLANES, uv3.shape[1], SUBLANES, LANES)
    return pl.pallas_call(
        functools.partial(_peer_kernel, alpha),
        grid=(n // tb,),
        in_specs=[
            pl.BlockSpec((tb, d), lambda i: (i, 0)),
            pl.BlockSpec((tb, d), lambda i: (jnp.minimum(i + 1, last), 0)),
            pl.BlockSpec((tb, d), lambda i: (jnp.minimum(i + 2, last), 0)),
            pl.BlockSpec(wq_bf.shape, fix),
            pl.BlockSpec(keys_bf.shape, lambda i: (0, 0, 0, 0)),
            pl.BlockSpec(memory_space=pl.ANY),
            pl.BlockSpec((1, d), fix),
            pl.BlockSpec((1, d), fix),
        ],
        out_specs=pl.BlockSpec((tb, d), lambda i: (i, 0)),
        out_shape=jax.ShapeDtypeStruct((n, d), F32),
        scratch_shapes=(
            [pltpu.VMEM(slot_shape, F32) for _ in range(PEER_ALLOCS)]
            + [pltpu.SemaphoreType.DMA((PEER_ALLOCS, PEER_SET)),
               pltpu.VMEM((tb, d), F32),
               pltpu.VMEM((PEER_SET, hk, LANES), F32),
               pltpu.VMEM((PEER_SET, hk, LANES), F32),
               pltpu.VMEM((tb, PEER_HEADS * PEER_QDIM), F32),
               pltpu.VMEM((hk, tb), F32),
               pltpu.VMEM((hk, tb), I32),
               pltpu.VMEM((PEER_RING, tb, hk), F32),
               pltpu.VMEM((tb, hk), I32),
               pltpu.SMEM((PEER_RING, tb, hk), I32),
               pltpu.SemaphoreType.DMA((1,))]),
        compiler_params=pltpu.CompilerParams(
            dimension_semantics=("arbitrary",), vmem_limit_bytes=VMEM_LIMIT),
        name="peer",
    )(h1, h1, h1, wq_bf, keys_bf, uv3, g.reshape(1, d), b.reshape(1, d))


def kernel(x, w_in, b_in, conv_w, conv_b, conv_ln_g, conv_ln_b, w_out, b_out, ln1_g, ln1_b,
           peer_wq, peer_keys, peer_u, peer_v, ln2_g, ln2_b):
    batch, seq, d = x.shape
    depth = w_in.shape[0]
    alpha = (2.0 * depth) ** 0.25
    slopes = 2.0 ** (-8.0 * jnp.arange(1, N_HEADS + 1, dtype=F32) / N_HEADS)
    h = x.reshape(batch * seq, d)
    for l in range(depth):
        u, qkv = _in_proj(h, w_in[l].astype(BF16), b_in[l])
        yc = _conv(u, conv_w[l], conv_b[l], conv_ln_g[l], conv_ln_b[l], batch, seq)
        ya = _attention(qkv, slopes, batch, seq)
        h1 = _out_proj(yc, ya, h, w_out[l].astype(BF16), b_out[l], ln1_g[l], ln1_b[l], alpha)
        uv3 = jnp.concatenate([peer_u[l], peer_v[l]], axis=1).reshape(-1, 2 * d // LANES, LANES)
        h = _peer(h1, peer_wq[l].astype(BF16), peer_keys[l].astype(BF16), uv3,
                  ln2_g[l], ln2_b[l], alpha)
    return h.reshape(batch, seq, d)
```

```python
import functools
import math

import jax
import jax.numpy as jnp
from jax import lax
from jax.experimental import pallas as pl
from jax.experimental.pallas import tpu as pltpu

F32 = jnp.float32
BF16 = jnp.bfloat16
I32 = jnp.int32

D_MODEL = 1024
D_CONV = D_MODEL // 2
D_ATTN = D_MODEL - D_CONV
HEAD_DIM = 64
N_HEADS = D_ATTN // HEAD_DIM
CONV_WIDTH = 31
DILATED_PATTERNS = ((128, 1), (512, 4), (2048, 16))
ATTN_BLOCK = 128
ATTN_UNROLL = 4
PEER_HEADS = 8
PEER_NKEYS = 128
PEER_QDIM = 256
PEER_TOPK = 16
LN_EPS = 1e-5
NEG_INF = -1e30

LANES = 128
SUBLANES = 8
VMEM_LIMIT = 56 * 1024 * 1024

PEER_SET = 8
PEER_ALLOCS = 4
PEER_AHEAD = 2 * PEER_SET
PEER_TOKENS = 256
PEER_RING = 3
PEER_PITCH = 17


def _ln(x, g, b):
    mu = jnp.mean(x, axis=-1, keepdims=True)
    xc = x - mu
    var = jnp.mean(xc * xc, axis=-1, keepdims=True)
    return xc * lax.rsqrt(var + LN_EPS) * g + b


def _in_proj_kernel(x_ref, w_ref, b_ref, u_ref, qkv_ref):
    x = x_ref[...].astype(BF16)
    nu = u_ref.shape[1]
    u_ref[...] = jnp.dot(x, w_ref[:, :nu], preferred_element_type=F32) + b_ref[:, :nu]
    qkv_ref[...] = jnp.dot(x, w_ref[:, nu:], preferred_element_type=F32) + b_ref[:, nu:]


def _in_proj(x2, w_bf, b, tm=256):
    n, d = x2.shape
    dout = w_bf.shape[1]
    nu = 2 * D_CONV
    return pl.pallas_call(
        _in_proj_kernel,
        grid=(n // tm,),
        in_specs=[
            pl.BlockSpec((tm, d), lambda i: (i, 0)),
            pl.BlockSpec((d, dout), lambda i: (0, 0)),
            pl.BlockSpec((1, dout), lambda i: (0, 0)),
        ],
        out_specs=[
            pl.BlockSpec((tm, nu), lambda i: (i, 0)),
            pl.BlockSpec((tm, dout - nu), lambda i: (i, 0)),
        ],
        out_shape=[
            jax.ShapeDtypeStruct((n, nu), F32),
            jax.ShapeDtypeStruct((n, dout - nu), F32),
        ],
        compiler_params=pltpu.CompilerParams(
            dimension_semantics=("arbitrary",), vmem_limit_bytes=VMEM_LIMIT),
        name="in_proj",
    )(x2, w_bf, b.reshape(1, dout))


CONV_ROWS = 64
CONV_PAD = 32


def _conv_kernel(u_ref, w_ref, cb_ref, g_ref, b_ref, o_ref, pad_ref):
    s = o_ref.shape[0]
    pad_ref[0:CONV_PAD, :] = jnp.zeros((CONV_PAD, D_CONV), F32)

    def glu(i, c):
        r0 = pl.multiple_of(i * 256, 256)
        a = u_ref[pl.ds(r0, 256), 0:D_CONV]
        gate = u_ref[pl.ds(r0, 256), D_CONV:2 * D_CONV]
        pad_ref[pl.ds(CONV_PAD + r0, 256), :] = a * jax.nn.sigmoid(gate)
        return c

    lax.fori_loop(0, s // 256, glu, 0)

    off = CONV_PAD - (CONV_WIDTH - 1)

    def tile(i, c):
        r0 = pl.multiple_of(i * CONV_ROWS, CONV_ROWS)
        acc = jnp.broadcast_to(cb_ref[...], (CONV_ROWS, D_CONV))
        win = pad_ref[pl.ds(r0, CONV_ROWS + CONV_PAD), :]
        for j in range(CONV_WIDTH):
            acc = acc + w_ref[j:j + 1, :] * win[off + j:off + j + CONV_ROWS, :]
        y = _ln(acc, g_ref[...], b_ref[...])
        o_ref[pl.ds(r0, CONV_ROWS), :] = (y * jax.nn.sigmoid(y)).astype(o_ref.dtype)
        return c

    lax.fori_loop(0, s // CONV_ROWS, tile, 0)


def _conv(u, conv_w, conv_b, g, b, batch, seq):
    n = u.shape[0]
    return pl.pallas_call(
        _conv_kernel,
        grid=(batch,),
        in_specs=[
            pl.BlockSpec((seq, 2 * D_CONV), lambda i: (i, 0)),
            pl.BlockSpec((CONV_WIDTH, D_CONV), lambda i: (0, 0)),
            pl.BlockSpec((1, D_CONV), lambda i: (0, 0)),
            pl.BlockSpec((1, D_CONV), lambda i: (0, 0)),
            pl.BlockSpec((1, D_CONV), lambda i: (0, 0)),
        ],
        out_specs=pl.BlockSpec((seq, D_CONV), lambda i: (i, 0)),
        out_shape=jax.ShapeDtypeStruct((n, D_CONV), BF16),
        scratch_shapes=[pltpu.VMEM((seq + CONV_PAD, D_CONV), F32)],
        compiler_params=pltpu.CompilerParams(
            dimension_semantics=("arbitrary",), vmem_limit_bytes=VMEM_LIMIT),
        name="conv",
    )(u, conv_w, conv_b.reshape(1, -1), g.reshape(1, -1), b.reshape(1, -1))


def _attn_kernel(slopes_ref, q_ref, k_ref, v_ref, o_ref, bias_ref, oacc_ref, m_ref, l_ref):
    blk = ATTN_BLOCK
    s_len = q_ref.shape[0]
    hp = pl.program_id(1)
    n_pat = len(DILATED_PATTERNS)

    lane = lax.broadcasted_iota(I32, (blk, LANES), 1)
    head0 = lane < HEAD_DIM
    lane2 = lax.broadcasted_iota(I32, (2 * blk, LANES), 1)
    head0_2 = lane2 < HEAD_DIM

    qi = lax.broadcasted_iota(I32, (blk, 2 * blk), 0)
    kj = lax.broadcasted_iota(I32, (blk, 2 * blk), 1)
    dist = qi + blk - kj
    is_prev = (kj < blk).astype(F32)
    for p, (window, dil) in enumerate(DILATED_PATTERNS):
        n_back = window // dil
        valid = (dist >= 0) & (dist <= n_back)
        dpos = (dist * dil).astype(F32)
        for hh in range(2):
            slope = slopes_ref[2 * hp + hh]
            bias_ref[p, hh] = jnp.where(valid, -slope * dpos, NEG_INF)

    def units(p, dil, specs):
        def rows(ref, start):
            if dil == 1:
                return ref[pl.ds(start, blk), :]
            return ref[pl.ds(start, blk, stride=dil), :]

        scores = []
        values = []
        for cur, prev, first in specs:
            qb = rows(q_ref, cur) * (HEAD_DIM ** -0.5)
            if prev is None:
                kc = rows(k_ref, cur)
                vc = rows(v_ref, cur)
                hmask = head0
            else:
                kc = jnp.concatenate([rows(k_ref, prev), rows(k_ref, cur)], axis=0)
                vc = jnp.concatenate([rows(v_ref, prev), rows(v_ref, cur)], axis=0)
                hmask = head0_2
                pen = jnp.where(first, NEG_INF, 0.0) * is_prev
            kb = kc.astype(BF16)
            for hh in range(2):
                sel = head0 if hh == 0 else jnp.logical_not(head0)
                vsel = hmask if hh == 0 else jnp.logical_not(hmask)
                qh = jnp.where(sel, qb, 0.0).astype(BF16)
                values.append(jnp.where(vsel, vc, 0.0).astype(BF16))
                sc = lax.dot_general(qh, kb, (((1,), (1,)), ((), ())),
                                     preferred_element_type=F32)
                if prev is None:
                    sc = sc + bias_ref[p, hh, :, blk:2 * blk]
                else:
                    sc = sc + bias_ref[p, hh] + pen
                scores.append(sc)
        probs = []
        stats = []
        for sc in scores:
            m = jnp.max(sc, axis=1, keepdims=True)
            e = jnp.exp(sc - m)
            stats.append((m, jnp.sum(e, axis=1, keepdims=True)))
            probs.append(e.astype(BF16))
        outs = [jnp.dot(e, vh, preferred_element_type=F32) for e, vh in zip(probs, values)]
        for n, (cur, prev, first) in enumerate(specs):
            (m0, l0), (m1, l1) = stats[2 * n], stats[2 * n + 1]
            if dil == 1:
                idx = pl.ds(cur, blk)
            else:
                idx = pl.ds(cur, blk, stride=dil)
            oacc_ref[p, idx, :] = outs[2 * n] + outs[2 * n + 1]
            m_ref[p, idx, :] = jnp.where(head0, jnp.broadcast_to(m0, (blk, LANES)),
                                         jnp.broadcast_to(m1, (blk, LANES)))
            l_ref[p, idx, :] = jnp.where(head0, jnp.broadcast_to(l0, (blk, LANES)),
                                         jnp.broadcast_to(l1, (blk, LANES)))

    for p, (window, dil) in enumerate(DILATED_PATTERNS):
        stream_len = s_len // dil
        nb = stream_len // blk

        if nb == 1:
            def body(r4, c, p=p, dil=dil):
                units(p, dil, [(r4 * ATTN_UNROLL + uu, None, None) for uu in range(ATTN_UNROLL)])
                return c
            lax.fori_loop(0, dil // ATTN_UNROLL, body, 0)
        else:
            def body(u4, c, p=p, dil=dil, nb=nb):
                specs = []
                for uu in range(ATTN_UNROLL):
                    u = u4 * ATTN_UNROLL + uu
                    r = u // nb
                    i = u - r * nb
                    cur = r + i * (blk * dil)
                    prev = r + jnp.maximum(i - 1, 0) * (blk * dil)
                    if dil == 1:
                        cur = pl.multiple_of(cur, blk)
                        prev = pl.multiple_of(prev, blk)
                    specs.append((cur, prev, i == 0))
                units(p, dil, specs)
                return c
            lax.fori_loop(0, dil * nb // ATTN_UNROLL, body, 0)

    def merge(i, c):
        r0 = pl.multiple_of(i * 256, 256)
        ms = [m_ref[p, pl.ds(r0, 256), :] for p in range(n_pat)]
        mx = functools.reduce(jnp.maximum, ms)
        num = None
        den = None
        for p in range(n_pat):
            w = jnp.exp(ms[p] - mx)
            tn = w * oacc_ref[p, pl.ds(r0, 256), :]
            td = w * l_ref[p, pl.ds(r0, 256), :]
            num = tn if num is None else num + tn
            den = td if den is None else den + td
        o_ref[pl.ds(r0, 256), :] = (num / den).astype(o_ref.dtype)
        return c

    lax.fori_loop(0, s_len // 256, merge, 0)


def _attention(qkv, slopes, batch, seq):
    n = qkv.shape[0]
    n_hp = D_ATTN // LANES
    n_pat = len(DILATED_PATTERNS)
    return pl.pallas_call(
        _attn_kernel,
        grid=(batch, n_hp),
        in_specs=[
            pl.BlockSpec(memory_space=pltpu.SMEM),
            pl.BlockSpec((seq, LANES), lambda b, h: (b, h)),
            pl.BlockSpec((seq, LANES), lambda b, h: (b, n_hp + h)),
            pl.BlockSpec((seq, LANES), lambda b, h: (b, 2 * n_hp + h)),
        ],
        out_specs=pl.BlockSpec((seq, LANES), lambda b, h: (b, h)),
        out_shape=jax.ShapeDtypeStruct((n, D_ATTN), BF16),
        scratch_shapes=[
            pltpu.VMEM((n_pat, 2, ATTN_BLOCK, 2 * ATTN_BLOCK), F32),
            pltpu.VMEM((n_pat, seq, LANES), F32),
            pltpu.VMEM((n_pat, seq, LANES), F32),
            pltpu.VMEM((n_pat, seq, LANES), F32),
        ],
        compiler_params=pltpu.CompilerParams(
            dimension_semantics=("arbitrary", "arbitrary"), vmem_limit_bytes=VMEM_LIMIT),
        name="attn",
    )(slopes, qkv, qkv, qkv)


def _out_proj_kernel(alpha, yc_ref, ya_ref, x_ref, wc_ref, wa_ref, b_ref, g_ref, be_ref, o_ref):
    y = jnp.dot(yc_ref[...], wc_ref[...], preferred_element_type=F32)
    y = y + jnp.dot(ya_ref[...], wa_ref[...], preferred_element_type=F32)
    y = y + b_ref[...]
    o_ref[...] = _ln(alpha * x_ref[...] + y, g_ref[...], be_ref[...])


def _out_proj(yc, ya, x2, w_out_bf, b_out, g, b, alpha, tm=256):
    n, d = x2.shape
    row = lambda i: (i, 0)
    fix = lambda i: (0, 0)
    return pl.pallas_call(
        functools.partial(_out_proj_kernel, alpha),
        grid=(n // tm,),
        in_specs=[
            pl.BlockSpec((tm, D_CONV), row),
            pl.BlockSpec((tm, D_ATTN), row),
            pl.BlockSpec((tm, d), row),
            pl.BlockSpec((D_CONV, d), fix),
            pl.BlockSpec((D_ATTN, d), fix),
            pl.BlockSpec((1, d), fix),
            pl.BlockSpec((1, d), fix),
            pl.BlockSpec((1, d), fix),
        ],
        out_specs=pl.BlockSpec((tm, d), row),
        out_shape=jax.ShapeDtypeStruct((n, d), F32),
        compiler_params=pltpu.CompilerParams(
            dimension_semantics=("arbitrary",), vmem_limit_bytes=VMEM_LIMIT),
        name="out_proj",
    )(yc, ya, x2, w_out_bf[:D_CONV], w_out_bf[D_CONV:], b_out.reshape(1, d),
      g.reshape(1, d), b.reshape(1, d))


def _route_fns(q_scr, keys_ref, gt_scr, it_scr):
    tiles = q_scr.shape[0] // LANES

    def where_of(item):
        h = item // tiles
        return h, pl.multiple_of((item - h * tiles) * LANES, LANES)

    k_top = PEER_TOPK
    nk = PEER_NKEYS
    lt = LANES
    iota_n = lax.broadcasted_iota(I32, (nk, lt), 0)
    iota_k = lax.broadcasted_iota(I32, (k_top, lt), 0)
    iota_8 = lax.broadcasted_iota(I32, (SUBLANES, lt), 0)
    neg = -jnp.inf

    half = k_top // 2
    blocks = ([(0, 0), (0, half)] + [(i, 0) for i in range(1, half)])
    pos = jnp.concatenate(
        [iota_8 + (i * k_top + j0) for (i, j0) in blocks] + [(iota_8 + half) * k_top], axis=0)

    def pin(x, mark):
        if mark is None:
            return x
        head = jnp.where(iota_8 == (mark | (1 << 30)), 0.0, x[:SUBLANES, :])
        return jnp.concatenate([head, x[SUBLANES:, :]], axis=0)

    def topk_init(item):
        h, lane0 = where_of(item)
        state = []
        for p in range(2):
            col = pl.multiple_of(h * PEER_QDIM + p * (PEER_QDIM // 2), PEER_QDIM // 2)
            q = q_scr[pl.ds(lane0, lt), pl.ds(col, PEER_QDIM // 2)].astype(BF16)
            s = lax.dot_general(keys_ref[h, p], q, (((1,), (1,)), ((), ())),
                                preferred_element_type=F32)
            state.append((s, jnp.zeros((k_top, lt), F32), jnp.zeros((k_top, lt), I32)))
        return state

    def topk_step(state, k, mark=None):
        out = []
        for s, ts, ti in state:
            s = pin(s, mark)
            m = jnp.max(s, axis=0, keepdims=True)
            am = jnp.min(jnp.where(s == m, iota_n, nk), axis=0, keepdims=True)
            s = jnp.where(iota_n == am, neg, s)
            out.append((s, jnp.where(iota_k == k, m, ts), jnp.where(iota_k == k, am, ti)))
        return out

    def topk_finish(state):
        return state[0][1], state[1][1], state[0][2], state[1][2]

    def comb_init(tops):
        sa, sb, ia, ib = tops
        cand = jnp.concatenate(
            [sa[i:i + 1, :] + sb[j0:j0 + SUBLANES, :] for (i, j0) in blocks]
            + [sa[half:, :] + sb[0:1, :]], axis=0)
        eid = jnp.concatenate(
            [ia[i:i + 1, :] * nk + ib[j0:j0 + SUBLANES, :] for (i, j0) in blocks]
            + [ia[half:, :] * nk + ib[0:1, :]], axis=0)
        return cand, eid, jnp.zeros((k_top, lt), F32), jnp.zeros((k_top, lt), I32)

    def comb_step(state, k, mark=None):
        cand, eid, bs, be = state
        cand = pin(cand, mark)
        m = jnp.max(cand, axis=0, keepdims=True)
        am = jnp.min(jnp.where(cand == m, pos, k_top * k_top), axis=0, keepdims=True)
        sel = pos == am
        e = jnp.max(jnp.where(sel, eid, -1), axis=0, keepdims=True)
        cand = jnp.where(sel, neg, cand)
        return cand, eid, jnp.where(iota_k == k, m, bs), jnp.where(iota_k == k, e, be)

    def comb_finish(item, state):
        h, lane0 = where_of(item)
        _, _, bs, be = state
        ex = jnp.exp(bs - bs[0:1, :])
        gate = ex / jnp.sum(ex, axis=0, keepdims=True)
        r0 = pl.multiple_of(h * k_top, k_top)
        gt_scr[pl.ds(r0, k_top), pl.ds(lane0, lt)] = gate
        it_scr[pl.ds(r0, k_top), pl.ds(lane0, lt)] = be

    return (topk_init, topk_step, topk_finish), (comb_init, comb_step, comb_finish)


def _peer_kernel(alpha, h_ref, h1_ref, h2_ref, wq_ref, keys_ref, uv_ref, lg_ref, lb_ref, o_ref,
                 *scratch):
    bufs = scratch[:PEER_ALLOCS]
    (sem, y_scr, acc_scr, wb_scr, q_scr, gt_scr, it_scr, g_ring, idx_stage, idx_ring,
     ring_sem) = scratch[PEER_ALLOCS:]
    tb, d = h_ref.shape
    hk = PEER_HEADS * PEER_TOPK
    n_lt = d // LANES
    n_kg = hk // SUBLANES
    row_tiles = uv_ref.shape[1]
    st = PEER_SET
    ahead_sets = PEER_AHEAD // st
    step = pl.program_id(0)
    cur = step % PEER_RING
    (tk_init, tk_step, tk_finish), (cb_init, cb_step, cb_finish) = _route_fns(
        q_scr, keys_ref, gt_scr, it_scr)
    n_items = PEER_HEADS * (tb // LANES)

    def route_source(src_ref):
        q_scr[...] = jnp.dot(src_ref[...].astype(BF16), wq_ref[...], preferred_element_type=F32)

    def ring_copy(slot):
        return pltpu.make_async_copy(idx_stage, idx_ring.at[slot], ring_sem.at[0])

    def route_publish(slot):
        g_ring[slot] = gt_scr[...].T
        idx_stage[...] = it_scr[...].T
        ring_copy(slot).start()

    def route_block(src_ref, slot):
        route_source(src_ref)

        def one(item, c):
            ts = tk_init(item)
            for k in range(PEER_TOPK):
                ts = tk_step(ts, k)
            cs = cb_init(tk_finish(ts))
            for k in range(PEER_TOPK):
                cs = cb_step(cs, k)
            cb_finish(item, cs)
            return c
        lax.fori_loop(0, n_items, one, 0)
        route_publish(slot)
        ring_copy(slot).wait()

    def issue_group(tok, alloc, slot, kg):
        la = tok + PEER_AHEAD
        over = la >= tb
        ring = jnp.where(over, (step + 1) % PEER_RING, cur)
        row = jnp.where(over, la - tb, la)
        for r in range(SUBLANES):
            e = idx_ring[ring, row, kg * SUBLANES + r]
            pltpu.make_async_copy(
                uv_ref.at[e],
                bufs[alloc].at[slot, pl.ds((kg * SUBLANES + r) * PEER_PITCH, row_tiles), :],
                sem.at[alloc, slot]).start(priority=r % 2)
        return e

    def wait(alloc, slot):
        done = bufs[alloc].at[slot, pl.ds(0, hk * row_tiles), :]
        pltpu.make_async_copy(done, done, sem.at[alloc, slot]).wait()

    def expert_tile(buf, j, kg, c):
        return buf[j, pl.ds(kg * SUBLANES * PEER_PITCH + c, SUBLANES, stride=PEER_PITCH), :]

    eye = (lax.broadcasted_iota(I32, (hk, hk), 0) == lax.broadcasted_iota(I32, (hk, hk), 1))
    ones_row = jnp.ones((SUBLANES, LANES), BF16)
    ones_sq = jnp.ones((hk, LANES), BF16)
    sub = lax.broadcasted_iota(I32, (SUBLANES, LANES), 0)

    def gate_chain(tok0):
        acc = acc_scr[...].reshape(st * hk, LANES).astype(BF16)
        sums = lax.dot_general(ones_row, acc, (((1,), (1,)), ((), ())),
                               preferred_element_type=F32)
        a = jnp.zeros((st, hk), F32)
        for j in range(st):
            a = jnp.where(sub == j, sums[:, j * hk:(j + 1) * hk], a)
        act = 0.5 * a * (1.0 + lax.erf(a * (2.0 ** -0.5)))
        w = act * g_ring[cur, pl.ds(tok0, st), :]
        diag = jnp.concatenate(
            [jnp.where(eye, jnp.broadcast_to(w[j:j + 1, :], (hk, hk)), 0.0).astype(BF16)
             for j in range(st)], axis=0)
        wb = jnp.dot(diag, ones_sq, preferred_element_type=F32)
        wb_scr[...] = wb.reshape(st, hk, LANES)

    @pl.when(step == 0)
    def _():
        route_block(h_ref, 0)
        route_block(h1_ref, 1)
        for a in range(ahead_sets):
            def first(j, c, a=a):
                for kg in range(n_kg):
                    issue_group(a * st + j - PEER_AHEAD, a, j, kg)
                return c
            lax.fori_loop(0, st, first, 0)

    @pl.when(step > 0)
    def _():
        ring_copy((step + 1) % PEER_RING).wait()

    route_source(h2_ref)
    body_tokens = PEER_ALLOCS * st
    items_per_body = n_items // (tb // body_tokens)

    def body(it, carry):
        base = pl.multiple_of(it * body_tokens, body_tokens)
        tops = None
        for q in range(PEER_ALLOCS):
            buf = bufs[q]
            nxt = (q + ahead_sets) % PEER_ALLOCS
            tok0 = base + q * st
            item = it * items_per_body + q // 2
            route = tk_init(item) if q % 2 == 0 else cb_init(tops)
            route_step = tk_step if q % 2 == 0 else cb_step
            for j in range(st):
                wait(q, j)

            for i in range(n_kg // 2):
                for j in range(st):
                    mark = issue_group(tok0 + j, nxt, j, i)
                    x_row = h_ref[pl.ds(tok0 + j, 1), :]
                    for g2 in range(2):
                        kg = 2 * i + g2
                        acc = None
                        for c2 in range(n_lt):
                            xc = jnp.broadcast_to(x_row[:, c2 * LANES:(c2 + 1) * LANES],
                                                  (SUBLANES, LANES))
                            term = expert_tile(buf, j, kg, c2) * xc
                            acc = term if acc is None else acc + term
                        acc_scr[j, kg * SUBLANES:(kg + 1) * SUBLANES, :] = acc
                route = route_step(route, i, mark)

            gate_chain(pl.multiple_of(tok0, st))

            for c2 in range(n_lt):
                for j in range(st):
                    mark = issue_group(tok0 + j, nxt, j, n_kg // 2 + c2)
                tile = jnp.zeros((SUBLANES, LANES), F32)
                for j in range(st):
                    part = None
                    for kg in range(n_kg):
                        t8 = (expert_tile(buf, j, kg, n_lt + c2)
                              * wb_scr[j, kg * SUBLANES:(kg + 1) * SUBLANES, :])
                        part = t8 if part is None else part + t8
                    row = jnp.sum(part, axis=0, keepdims=True)
                    tile = jnp.where(sub == j, jnp.broadcast_to(row, (SUBLANES, LANES)), tile)
                y_scr[pl.ds(pl.multiple_of(tok0, st), st), c2 * LANES:(c2 + 1) * LANES] = tile
                route = route_step(route, n_kg // 2 + c2, mark)

            if q % 2 == 0:
                tops = tk_finish(route)
            else:
                cb_finish(item, route)
        return carry

    lax.fori_loop(0, tb // body_tokens, body, 0)
    route_publish((step + 2) % PEER_RING)

    @pl.when(step == pl.num_programs(0) - 1)
    def _():
        ring_copy((step + 2) % PEER_RING).wait()
        for t in range(PEER_AHEAD):
            wait((t // st) % PEER_ALLOCS, t % st)

    o_ref[...] = _ln(alpha * h_ref[...] + y_scr[...], lg_ref[...], lb_ref[...])


def _peer(h1, wq_bf, keys_bf, uv3, g, b, alpha):
    n, d = h1.shape
    tb = PEER_TOKENS
    hk = PEER_HEADS * PEER_TOPK
    assert SUBLANES == PEER_SET and PEER_AHEAD % PEER_SET == 0
    assert 0 < PEER_AHEAD // PEER_SET < PEER_ALLOCS - 1 and PEER_AHEAD <= tb
    assert hk // SUBLANES == 2 * (d // LANES) and tb % LANES == 0 and hk == LANES
    assert tb % (PEER_ALLOCS * PEER_SET) == 0 and n % tb == 0
    assert PEER_ALLOCS * (tb // (PEER_ALLOCS * PEER_SET)) == 2 * PEER_HEADS * (tb // LANES)
    last = n // tb - 1
    fix = lambda i: (0, 0)
    assert uv3.shape[1] < PEER_PITCH
    slot_shape = (PEER_SET, hk * PEER_PITCH, LANES)
    return pl.pallas_call(
        functools.partial(_peer_kernel, alpha),
        grid=(n // tb,),
        in_specs=[
            pl.BlockSpec((tb, d), lambda i: (i, 0)),
            pl.BlockSpec((tb, d), lambda i: (jnp.minimum(i + 1, last), 0)),
            pl.BlockSpec((tb, d), lambda i: (jnp.minimum(i + 2, last), 0)),
            pl.BlockSpec(wq_bf.shape, fix),
            pl.BlockSpec(keys_bf.shape, lambda i: (0, 0, 0, 0)),
            pl.BlockSpec(memory_space=pl.ANY),
            pl.BlockSpec((1, d), fix),
            pl.BlockSpec((1, d), fix),
        ],
        out_specs=pl.BlockSpec((tb, d), lambda i: (i, 0)),
        out_shape=jax.ShapeDtypeStruct((n, d), F32),
        scratch_shapes=(
            [pltpu.VMEM(slot_shape, F32) for _ in range(PEER_ALLOCS)]
            + [pltpu.SemaphoreType.DMA((PEER_ALLOCS, PEER_SET)),
               pltpu.VMEM((tb, d), F32),
               pltpu.VMEM((PEER_SET, hk, LANES), F32),
               pltpu.VMEM((PEER_SET, hk, LANES), F32),
               pltpu.VMEM((tb, PEER_HEADS * PEER_QDIM), F32),
               pltpu.VMEM((hk, tb), F32),
               pltpu.VMEM((hk, tb), I32),
               pltpu.VMEM((PEER_RING, tb, hk), F32),
               pltpu.VMEM((tb, hk), I32),
               pltpu.SMEM((PEER_RING, tb, hk), I32),
               pltpu.SemaphoreType.DMA((1,))]),
        compiler_params=pltpu.CompilerParams(
            dimension_semantics=("arbitrary",), vmem_limit_bytes=VMEM_LIMIT),
        name="peer",
    )(h1, h1, h1, wq_bf, keys_bf, uv3, g.reshape(1, d), b.reshape(1, d))


def kernel(x, w_in, b_in, conv_w, conv_b, conv_ln_g, conv_ln_b, w_out, b_out, ln1_g, ln1_b,
           peer_wq, peer_keys, peer_u, peer_v, ln2_g, ln2_b):
    batch, seq, d = x.shape
    depth = w_in.shape[0]
    alpha = (2.0 * depth) ** 0.25
    slopes = 2.0 ** (-8.0 * jnp.arange(1, N_HEADS + 1, dtype=F32) / N_HEADS)
    h = x.reshape(batch * seq, d)
    for l in range(depth):
        u, qkv = _in_proj(h, w_in[l].astype(BF16), b_in[l])
        yc = _conv(u, conv_w[l], conv_b[l], conv_ln_g[l], conv_ln_b[l], batch, seq)
        ya = _attention(qkv, slopes, batch, seq)
        h1 = _out_proj(yc, ya, h, w_out[l].astype(BF16), b_out[l], ln1_g[l], ln1_b[l], alpha)
        uv3 = jnp.concatenate([peer_u[l], peer_v[l]], axis=1).reshape(-1, 2 * d // LANES, LANES)
        h = _peer(h1, peer_wq[l].astype(BF16), peer_keys[l].astype(BF16), uv3,
                  ln2_g[l], ln2_b[l], alpha)
    return h.reshape(batch, seq, d)
```

```python
import functools
import math

import jax
import jax.numpy as jnp
from jax import lax
from jax.experimental import pallas as pl
from jax.experimental.pallas import tpu as pltpu

F32 = jnp.float32
BF16 = jnp.bfloat16
I32 = jnp.int32

D_MODEL = 1024
D_CONV = D_MODEL // 2
D_ATTN = D_MODEL - D_CONV
HEAD_DIM = 64
N_HEADS = D_ATTN // HEAD_DIM
CONV_WIDTH = 31
DILATED_PATTERNS = ((128, 1), (512, 4), (2048, 16))
ATTN_BLOCK = 128
ATTN_UNROLL = 8
PEER_HEADS = 8
PEER_NKEYS = 128
PEER_QDIM = 256
PEER_TOPK = 16
LN_EPS = 1e-5
NEG_INF = -1e30

LANES = 128
SUBLANES = 8
VMEM_LIMIT = 56 * 1024 * 1024

PEER_SET = 8
PEER_ALLOCS = 4
PEER_AHEAD = 2 * PEER_SET
PEER_TOKENS = 256
PEER_RING = 3
PEER_PITCH = 17


def _ln(x, g, b):
    mu = jnp.mean(x, axis=-1, keepdims=True)
    xc = x - mu
    var = jnp.mean(xc * xc, axis=-1, keepdims=True)
    return xc * lax.rsqrt(var + LN_EPS) * g + b


def _in_proj_kernel(x_ref, w_ref, b_ref, u_ref, qkv_ref):
    x = x_ref[...].astype(BF16)
    nu = u_ref.shape[1]
    u_ref[...] = jnp.dot(x, w_ref[:, :nu], preferred_element_type=F32) + b_ref[:, :nu]
    qkv_ref[...] = jnp.dot(x, w_ref[:, nu:], preferred_element_type=F32) + b_ref[:, nu:]


def _in_proj(x2, w_bf, b, tm=256):
    n, d = x2.shape
    dout = w_bf.shape[1]
    nu = 2 * D_CONV
    return pl.pallas_call(
        _in_proj_kernel,
        grid=(n // tm,),
        in_specs=[
            pl.BlockSpec((tm, d), lambda i: (i, 0)),
            pl.BlockSpec((d, dout), lambda i: (0, 0)),
            pl.BlockSpec((1, dout), lambda i: (0, 0)),
        ],
        out_specs=[
            pl.BlockSpec((tm, nu), lambda i: (i, 0)),
            pl.BlockSpec((tm, dout - nu), lambda i: (i, 0)),
        ],
        out_shape=[
            jax.ShapeDtypeStruct((n, nu), F32),
            jax.ShapeDtypeStruct((n, dout - nu), F32),
        ],
        compiler_params=pltpu.CompilerParams(
            dimension_semantics=("arbitrary",), vmem_limit_bytes=VMEM_LIMIT),
        name="in_proj",
    )(x2, w_bf, b.reshape(1, dout))


CONV_ROWS = 64
CONV_PAD = 32


def _conv_kernel(u_ref, w_ref, cb_ref, g_ref, b_ref, o_ref, pad_ref):
    s = o_ref.shape[0]
    pad_ref[0:CONV_PAD, :] = jnp.zeros((CONV_PAD, D_CONV), F32)

    def glu(i, c):
        r0 = pl.multiple_of(i * 256, 256)
        a = u_ref[pl.ds(r0, 256), 0:D_CONV]
        gate = u_ref[pl.ds(r0, 256), D_CONV:2 * D_CONV]
        pad_ref[pl.ds(CONV_PAD + r0, 256), :] = a * jax.nn.sigmoid(gate)
        return c

    lax.fori_loop(0, s // 256, glu, 0)

    off = CONV_PAD - (CONV_WIDTH - 1)

    def tile(i, c):
        r0 = pl.multiple_of(i * CONV_ROWS, CONV_ROWS)
        acc = jnp.broadcast_to(cb_ref[...], (CONV_ROWS, D_CONV))
        win = pad_ref[pl.ds(r0, CONV_ROWS + CONV_PAD), :]
        for j in range(CONV_WIDTH):
            acc = acc + w_ref[j:j + 1, :] * win[off + j:off + j + CONV_ROWS, :]
        y = _ln(acc, g_ref[...], b_ref[...])
        o_ref[pl.ds(r0, CONV_ROWS), :] = (y * jax.nn.sigmoid(y)).astype(o_ref.dtype)
        return c

    lax.fori_loop(0, s // CONV_ROWS, tile, 0)


def _conv(u, conv_w, conv_b, g, b, batch, seq):
    n = u.shape[0]
    return pl.pallas_call(
        _conv_kernel,
        grid=(batch,),
        in_specs=[
            pl.BlockSpec((seq, 2 * D_CONV), lambda i: (i, 0)),
            pl.BlockSpec((CONV_WIDTH, D_CONV), lambda i: (0, 0)),
            pl.BlockSpec((1, D_CONV), lambda i: (0, 0)),
            pl.BlockSpec((1, D_CONV), lambda i: (0, 0)),
            pl.BlockSpec((1, D_CONV), lambda i: (0, 0)),
        ],
        out_specs=pl.BlockSpec((seq, D_CONV), lambda i: (i, 0)),
        out_shape=jax.ShapeDtypeStruct((n, D_CONV), BF16),
        scratch_shapes=[pltpu.VMEM((seq + CONV_PAD, D_CONV), F32)],
        compiler_params=pltpu.CompilerParams(
            dimension_semantics=("arbitrary",), vmem_limit_bytes=VMEM_LIMIT),
        name="conv",
    )(u, conv_w, conv_b.reshape(1, -1), g.reshape(1, -1), b.reshape(1, -1))


def _attn_kernel(slopes_ref, q_ref, k_ref, v_ref, o_ref, bias_ref, oacc_ref, m_ref, l_ref):
    blk = ATTN_BLOCK
    s_len = q_ref.shape[0]
    hp = pl.program_id(1)
    n_pat = len(DILATED_PATTERNS)

    lane = lax.broadcasted_iota(I32, (blk, LANES), 1)
    head0 = lane < HEAD_DIM
    lane2 = lax.broadcasted_iota(I32, (2 * blk, LANES), 1)
    head0_2 = lane2 < HEAD_DIM

    qi = lax.broadcasted_iota(I32, (blk, 2 * blk), 0)
    kj = lax.broadcasted_iota(I32, (blk, 2 * blk), 1)
    dist = qi + blk - kj
    is_prev = (kj < blk).astype(F32)
    for p, (window, dil) in enumerate(DILATED_PATTERNS):
        n_back = window // dil
        valid = (dist >= 0) & (dist <= n_back)
        dpos = (dist * dil).astype(F32)
        for hh in range(2):
            slope = slopes_ref[2 * hp + hh]
            bias_ref[p, hh] = jnp.where(valid, -slope * dpos, NEG_INF)

    def units(p, dil, specs):
        def rows(ref, start):
            if dil == 1:
                return ref[pl.ds(start, blk), :]
            return ref[pl.ds(start, blk, stride=dil), :]

        scores = []
        values = []
        for cur, prev, first in specs:
            qb = rows(q_ref, cur) * (HEAD_DIM ** -0.5)
            if prev is None:
                kc = rows(k_ref, cur)
                vc = rows(v_ref, cur)
                hmask = head0
            else:
                kc = jnp.concatenate([rows(k_ref, prev), rows(k_ref, cur)], axis=0)
                vc = jnp.concatenate([rows(v_ref, prev), rows(v_ref, cur)], axis=0)
                hmask = head0_2
                pen = jnp.where(first, NEG_INF, 0.0) * is_prev
            kb = kc.astype(BF16)
            for hh in range(2):
                sel = head0 if hh == 0 else jnp.logical_not(head0)
                vsel = hmask if hh == 0 else jnp.logical_not(hmask)
                qh = jnp.where(sel, qb, 0.0).astype(BF16)
                values.append(jnp.where(vsel, vc, 0.0).astype(BF16))
                sc = lax.dot_general(qh, kb, (((1,), (1,)), ((), ())),
                                     preferred_element_type=F32)
                if prev is None:
                    sc = sc + bias_ref[p, hh, :, blk:2 * blk]
                else:
                    sc = sc + bias_ref[p, hh] + pen
                scores.append(sc)
        probs = []
        stats = []
        for sc in scores:
            m = jnp.max(sc, axis=1, keepdims=True)
            e = jnp.exp(sc - m)
            stats.append((m, jnp.sum(e, axis=1, keepdims=True)))
            probs.append(e.astype(BF16))
        outs = [jnp.dot(e, vh, preferred_element_type=F32) for e, vh in zip(probs, values)]
        for n, (cur, prev, first) in enumerate(specs):
            (m0, l0), (m1, l1) = stats[2 * n], stats[2 * n + 1]
            if dil == 1:
                idx = pl.ds(cur, blk)
            else:
                idx = pl.ds(cur, blk, stride=dil)
            oacc_ref[p, idx, :] = outs[2 * n] + outs[2 * n + 1]
            m_ref[p, idx, :] = jnp.where(head0, jnp.broadcast_to(m0, (blk, LANES)),
                                         jnp.broadcast_to(m1, (blk, LANES)))
            l_ref[p, idx, :] = jnp.where(head0, jnp.broadcast_to(l0, (blk, LANES)),
                                         jnp.broadcast_to(l1, (blk, LANES)))

    for p, (window, dil) in enumerate(DILATED_PATTERNS):
        stream_len = s_len // dil
        nb = stream_len // blk

        if nb == 1:
            def body(r4, c, p=p, dil=dil):
                units(p, dil, [(r4 * ATTN_UNROLL + uu, None, None) for uu in range(ATTN_UNROLL)])
                return c
            lax.fori_loop(0, dil // ATTN_UNROLL, body, 0)
        else:
            def body(u4, c, p=p, dil=dil, nb=nb):
                specs = []
                for uu in range(ATTN_UNROLL):
                    u = u4 * ATTN_UNROLL + uu
                    r = u // nb
                    i = u - r * nb
                    cur = r + i * (blk * dil)
                    prev = r + jnp.maximum(i - 1, 0) * (blk * dil)
                    if dil == 1:
                        cur = pl.multiple_of(cur, blk)
                        prev = pl.multiple_of(prev, blk)
                    specs.append((cur, prev, i == 0))
                units(p, dil, specs)
                return c
            lax.fori_loop(0, dil * nb // ATTN_UNROLL, body, 0)

    def merge(i, c):
        r0 = pl.multiple_of(i * 256, 256)
        ms = [m_ref[p, pl.ds(r0, 256), :] for p in range(n_pat)]
        mx = functools.reduce(jnp.maximum, ms)
        num = None
        den = None
        for p in range(n_pat):
            w = jnp.exp(ms[p] - mx)
            tn = w * oacc_ref[p, pl.ds(r0, 256), :]
            td = w * l_ref[p, pl.ds(r0, 256), :]
            num = tn if num is None else num + tn
            den = td if den is None else den + td
        o_ref[pl.ds(r0, 256), :] = (num / den).astype(o_ref.dtype)
        return c

    lax.fori_loop(0, s_len // 256, merge, 0)


def _attention(qkv, slopes, batch, seq):
    n = qkv.shape[0]
    n_hp = D_ATTN // LANES
    n_pat = len(DILATED_PATTERNS)
    return pl.pallas_call(
        _attn_kernel,
        grid=(batch, n_hp),
        in_specs=[
            pl.BlockSpec(memory_space=pltpu.SMEM),
            pl.BlockSpec((seq, LANES), lambda b, h: (b, h)),
            pl.BlockSpec((seq, LANES), lambda b, h: (b, n_hp + h)),
            pl.BlockSpec((seq, LANES), lambda b, h: (b, 2 * n_hp + h)),
        ],
        out_specs=pl.BlockSpec((seq, LANES), lambda b, h: (b, h)),
        out_shape=jax.ShapeDtypeStruct((n, D_ATTN), BF16),
        scratch_shapes=[
            pltpu.VMEM((n_pat, 2, ATTN_BLOCK, 2 * ATTN_BLOCK), F32),
            pltpu.VMEM((n_pat, seq, LANES), F32),
            pltpu.VMEM((n_pat, seq, LANES), F32),
            pltpu.VMEM((n_pat, seq, LANES), F32),
        ],
        compiler_params=pltpu.CompilerParams(
            dimension_semantics=("arbitrary", "arbitrary"), vmem_limit_bytes=VMEM_LIMIT),
        name="attn",
    )(slopes, qkv, qkv, qkv)


def _out_proj_kernel(alpha, yc_ref, ya_ref, x_ref, wc_ref, wa_ref, b_ref, g_ref, be_ref, o_ref):
    y = jnp.dot(yc_ref[...], wc_ref[...], preferred_element_type=F32)
    y = y + jnp.dot(ya_ref[...], wa_ref[...], preferred_element_type=F32)
    y = y + b_ref[...]
    o_ref[...] = _ln(alpha * x_ref[...] + y, g_ref[...], be_ref[...])


def _out_proj(yc, ya, x2, w_out_bf, b_out, g, b, alpha, tm=256):
    n, d = x2.shape
    row = lambda i: (i, 0)
    fix = lambda i: (0, 0)
    return pl.pallas_call(
        functools.partial(_out_proj_kernel, alpha),
        grid=(n // tm,),
        in_specs=[
            pl.BlockSpec((tm, D_CONV), row),
            pl.BlockSpec((tm, D_ATTN), row),
            pl.BlockSpec((tm, d), row),
            pl.BlockSpec((D_CONV, d), fix),
            pl.BlockSpec((D_ATTN, d), fix),
            pl.BlockSpec((1, d), fix),
            pl.BlockSpec((1, d), fix),
            pl.BlockSpec((1, d), fix),
        ],
        out_specs=pl.BlockSpec((tm, d), row),
        out_shape=jax.ShapeDtypeStruct((n, d), F32),
        compiler_params=pltpu.CompilerParams(
            dimension_semantics=("arbitrary",), vmem_limit_bytes=VMEM_LIMIT),
        name="out_proj",
    )(yc, ya, x2, w_out_bf[:D_CONV], w_out_bf[D_CONV:], b_out.reshape(1, d),
      g.reshape(1, d), b.reshape(1, d))


def _route_fns(q_scr, keys_ref, gt_scr, it_scr):
    tiles = q_scr.shape[0] // LANES

    def where_of(item):
        h = item // tiles
        return h, pl.multiple_of((item - h * tiles) * LANES, LANES)

    k_top = PEER_TOPK
    nk = PEER_NKEYS
    lt = LANES
    iota_n = lax.broadcasted_iota(I32, (nk, lt), 0)
    iota_k = lax.broadcasted_iota(I32, (k_top, lt), 0)
    iota_8 = lax.broadcasted_iota(I32, (SUBLANES, lt), 0)
    neg = -jnp.inf

    half = k_top // 2
    blocks = ([(0, 0), (0, half)] + [(i, 0) for i in range(1, half)])
    pos = jnp.concatenate(
        [iota_8 + (i * k_top + j0) for (i, j0) in blocks] + [(iota_8 + half) * k_top], axis=0)

    def pin(x, mark):
        if mark is None:
            return x
        head = jnp.where(iota_8 == (mark | (1 << 30)), 0.0, x[:SUBLANES, :])
        return jnp.concatenate([head, x[SUBLANES:, :]], axis=0)

    def topk_init(item):
        h, lane0 = where_of(item)
        state = []
        for p in range(2):
            col = pl.multiple_of(h * PEER_QDIM + p * (PEER_QDIM // 2), PEER_QDIM // 2)
            q = q_scr[pl.ds(lane0, lt), pl.ds(col, PEER_QDIM // 2)].astype(BF16)
            s = lax.dot_general(keys_ref[h, p], q, (((1,), (1,)), ((), ())),
                                preferred_element_type=F32)
            state.append((s, jnp.zeros((k_top, lt), F32), jnp.zeros((k_top, lt), I32)))
        return state

    def topk_step(state, k, mark=None):
        out = []
        for s, ts, ti in state:
            s = pin(s, mark)
            m = jnp.max(s, axis=0, keepdims=True)
            am = jnp.min(jnp.where(s == m, iota_n, nk), axis=0, keepdims=True)
            s = jnp.where(iota_n == am, neg, s)
            out.append((s, jnp.where(iota_k == k, m, ts), jnp.where(iota_k == k, am, ti)))
        return out

    def topk_finish(state):
        return state[0][1], state[1][1], state[0][2], state[1][2]

    def comb_init(tops):
        sa, sb, ia, ib = tops
        cand = jnp.concatenate(
            [sa[i:i + 1, :] + sb[j0:j0 + SUBLANES, :] for (i, j0) in blocks]
            + [sa[half:, :] + sb[0:1, :]], axis=0)
        eid = jnp.concatenate(
            [ia[i:i + 1, :] * nk + ib[j0:j0 + SUBLANES, :] for (i, j0) in blocks]
            + [ia[half:, :] * nk + ib[0:1, :]], axis=0)
        return cand, eid, jnp.zeros((k_top, lt), F32), jnp.zeros((k_top, lt), I32)

    def comb_step(state, k, mark=None):
        cand, eid, bs, be = state
        cand = pin(cand, mark)
        m = jnp.max(cand, axis=0, keepdims=True)
        am = jnp.min(jnp.where(cand == m, pos, k_top * k_top), axis=0, keepdims=True)
        sel = pos == am
        e = jnp.max(jnp.where(sel, eid, -1), axis=0, keepdims=True)
        cand = jnp.where(sel, neg, cand)
        return cand, eid, jnp.where(iota_k == k, m, bs), jnp.where(iota_k == k, e, be)

    def comb_finish(item, state):
        h, lane0 = where_of(item)
        _, _, bs, be = state
        ex = jnp.exp(bs - bs[0:1, :])
        gate = ex / jnp.sum(ex, axis=0, keepdims=True)
        r0 = pl.multiple_of(h * k_top, k_top)
        gt_scr[pl.ds(r0, k_top), pl.ds(lane0, lt)] = gate
        it_scr[pl.ds(r0, k_top), pl.ds(lane0, lt)] = be

    return (topk_init, topk_step, topk_finish), (comb_init, comb_step, comb_finish)


def _peer_kernel(alpha, h_ref, h1_ref, h2_ref, wq_ref, keys_ref, uv_ref, lg_ref, lb_ref, o_ref,
                 *scratch):
    bufs = scratch[:PEER_ALLOCS]
    (sem, y_scr, acc_scr, wb_scr, q_scr, gt_scr, it_scr, g_ring, idx_stage, idx_ring,
     ring_sem) = scratch[PEER_ALLOCS:]
    tb, d = h_ref.shape
    hk = PEER_HEADS * PEER_TOPK
    n_lt = d // LANES
    n_kg = hk // SUBLANES
    row_tiles = uv_ref.shape[1]
    st = PEER_SET
    ahead_sets = PEER_AHEAD // st
    step = pl.program_id(0)
    cur = step % PEER_RING
    (tk_init, tk_step, tk_finish), (cb_init, cb_step, cb_finish) = _route_fns(
        q_scr, keys_ref, gt_scr, it_scr)
    n_items = PEER_HEADS * (tb // LANES)

    def route_source(src_ref):
        q_scr[...] = jnp.dot(src_ref[...].astype(BF16), wq_ref[...], preferred_element_type=F32)

    def ring_copy(slot):
        return pltpu.make_async_copy(idx_stage, idx_ring.at[slot], ring_sem.at[0])

    def route_publish(slot):
        g_ring[slot] = gt_scr[...].T
        idx_stage[...] = it_scr[...].T
        ring_copy(slot).start()

    def route_block(src_ref, slot):
        route_source(src_ref)

        def one(item, c):
            ts = tk_init(item)
            for k in range(PEER_TOPK):
                ts = tk_step(ts, k)
            cs = cb_init(tk_finish(ts))
            for k in range(PEER_TOPK):
                cs = cb_step(cs, k)
            cb_finish(item, cs)
            return c
        lax.fori_loop(0, n_items, one, 0)
        route_publish(slot)
        ring_copy(slot).wait()

    def issue_group(tok, alloc, slot, kg):
        la = tok + PEER_AHEAD
        over = la >= tb
        ring = jnp.where(over, (step + 1) % PEER_RING, cur)
        row = jnp.where(over, la - tb, la)
        for r in range(SUBLANES):
            e = idx_ring[ring, row, kg * SUBLANES + r]
            pltpu.make_async_copy(
                uv_ref.at[e],
                bufs[alloc].at[slot, pl.ds((kg * SUBLANES + r) * PEER_PITCH, row_tiles), :],
                sem.at[alloc, slot]).start(priority=r % 2)
        return e

    def wait(alloc, slot):
        done = bufs[alloc].at[slot, pl.ds(0, hk * row_tiles), :]
        pltpu.make_async_copy(done, done, sem.at[alloc, slot]).wait()

    def expert_tile(buf, j, kg, c):
        return buf[j, pl.ds(kg * SUBLANES * PEER_PITCH + c, SUBLANES, stride=PEER_PITCH), :]

    eye = (lax.broadcasted_iota(I32, (hk, hk), 0) == lax.broadcasted_iota(I32, (hk, hk), 1))
    ones_row = jnp.ones((SUBLANES, LANES), BF16)
    ones_sq = jnp.ones((hk, LANES), BF16)
    sub = lax.broadcasted_iota(I32, (SUBLANES, LANES), 0)

    def gate_chain(tok0):
        acc = acc_scr[...].reshape(st * hk, LANES).astype(BF16)
        sums = lax.dot_general(ones_row, acc, (((1,), (1,)), ((), ())),
                               preferred_element_type=F32)
        a = jnp.zeros((st, hk), F32)
        for j in range(st):
            a = jnp.where(sub == j, sums[:, j * hk:(j + 1) * hk], a)
        act = 0.5 * a * (1.0 + lax.erf(a * (2.0 ** -0.5)))
        w = act * g_ring[cur, pl.ds(tok0, st), :]
        diag = jnp.concatenate(
            [jnp.where(eye, jnp.broadcast_to(w[j:j + 1, :], (hk, hk)), 0.0).astype(BF16)
             for j in range(st)], axis=0)
        wb = jnp.dot(diag, ones_sq, preferred_element_type=F32)
        wb_scr[...] = wb.reshape(st, hk, LANES)

    @pl.when(step == 0)
    def _():
        route_block(h_ref, 0)
        route_block(h1_ref, 1)
        for a in range(ahead_sets):
            def first(j, c, a=a):
                for kg in range(n_kg):
                    issue_group(a * st + j - PEER_AHEAD, a, j, kg)
                return c
            lax.fori_loop(0, st, first, 0)

    @pl.when(step > 0)
    def _():
        ring_copy((step + 1) % PEER_RING).wait()

    route_source(h2_ref)
    body_tokens = PEER_ALLOCS * st
    items_per_body = n_items // (tb // body_tokens)

    def body(it, carry):
        base = pl.multiple_of(it * body_tokens, body_tokens)
        tops = None
        for q in range(PEER_ALLOCS):
            buf = bufs[q]
            nxt = (q + ahead_sets) % PEER_ALLOCS
            tok0 = base + q * st
            item = it * items_per_body + q // 2
            route = tk_init(item) if q % 2 == 0 else cb_init(tops)
            route_step = tk_step if q % 2 == 0 else cb_step
            for j in range(st):
                wait(q, j)

            for i in range(n_kg // 2):
                for j in range(st):
                    mark = issue_group(tok0 + j, nxt, j, i)
                    x_row = h_ref[pl.ds(tok0 + j, 1), :]
                    for g2 in range(2):
                        kg = 2 * i + g2
                        acc = None
                        for c2 in range(n_lt):
                            xc = jnp.broadcast_to(x_row[:, c2 * LANES:(c2 + 1) * LANES],
                                                  (SUBLANES, LANES))
                            term = expert_tile(buf, j, kg, c2) * xc
                            acc = term if acc is None else acc + term
                        acc_scr[j, kg * SUBLANES:(kg + 1) * SUBLANES, :] = acc
                route = route_step(route, i, mark)

            gate_chain(pl.multiple_of(tok0, st))

            for c2 in range(n_lt):
                for j in range(st):
                    mark = issue_group(tok0 + j, nxt, j, n_kg // 2 + c2)
                tile = jnp.zeros((SUBLANES, LANES), F32)
                for j in range(st):
                    part = None
                    for kg in range(n_kg):
                        t8 = (expert_tile(buf, j, kg, n_lt + c2)
                              * wb_scr[j, kg * SUBLANES:(kg + 1) * SUBLANES, :])
                        part = t8 if part is None else part + t8
                    row = jnp.sum(part, axis=0, keepdims=True)
                    tile = jnp.where(sub == j, jnp.broadcast_to(row, (SUBLANES, LANES)), tile)
                y_scr[pl.ds(pl.multiple_of(tok0, st), st), c2 * LANES:(c2 + 1) * LANES] = tile
                route = route_step(route, n_kg // 2 + c2, mark)

            if q % 2 == 0:
                tops = tk_finish(route)
            else:
                cb_finish(item, route)
        return carry

    lax.fori_loop(0, tb // body_tokens, body, 0)
    route_publish((step + 2) % PEER_RING)

    @pl.when(step == pl.num_programs(0) - 1)
    def _():
        ring_copy((step + 2) % PEER_RING).wait()
        for t in range(PEER_AHEAD):
            wait((t // st) % PEER_ALLOCS, t % st)

    o_ref[...] = _ln(alpha * h_ref[...] + y_scr[...], lg_ref[...], lb_ref[...])


def _peer(h1, wq_bf, keys_bf, uv3, g, b, alpha):
    n, d = h1.shape
    tb = PEER_TOKENS
    hk = PEER_HEADS * PEER_TOPK
    assert SUBLANES == PEER_SET and PEER_AHEAD % PEER_SET == 0
    assert 0 < PEER_AHEAD // PEER_SET < PEER_ALLOCS - 1 and PEER_AHEAD <= tb
    assert hk // SUBLANES == 2 * (d // LANES) and tb % LANES == 0 and hk == LANES
    assert tb % (PEER_ALLOCS * PEER_SET) == 0 and n % tb == 0
    assert PEER_ALLOCS * (tb // (PEER_ALLOCS * PEER_SET)) == 2 * PEER_HEADS * (tb // LANES)
    last = n // tb - 1
    fix = lambda i: (0, 0)
    assert uv3.shape[1] < PEER_PITCH
    slot_shape = (PEER_SET, hk * PEER_PITCH, LANES)
    return pl.pallas_call(
        functools.partial(_peer_kernel, alpha),
        grid=(n // tb,),
        in_specs=[
            pl.BlockSpec((tb, d), lambda i: (i, 0)),
            pl.BlockSpec((tb, d), lambda i: (jnp.minimum(i + 1, last), 0)),
            pl.BlockSpec((tb, d), lambda i: (jnp.minimum(i + 2, last), 0)),
            pl.BlockSpec(wq_bf.shape, fix),
            pl.BlockSpec(keys_bf.shape, lambda i: (0, 0, 0, 0)),
            pl.BlockSpec(memory_space=pl.ANY),
            pl.BlockSpec((1, d), fix),
            pl.BlockSpec((1, d), fix),
        ],
        out_specs=pl.BlockSpec((tb, d), lambda i: (i, 0)),
        out_shape=jax.ShapeDtypeStruct((n, d), F32),
        scratch_shapes=(
            [pltpu.VMEM(slot_shape, F32) for _ in range(PEER_ALLOCS)]
            + [pltpu.SemaphoreType.DMA((PEER_ALLOCS, PEER_SET)),
               pltpu.VMEM((tb, d), F32),
               pltpu.VMEM((PEER_SET, hk, LANES), F32),
               pltpu.VMEM((PEER_SET, hk, LANES), F32),
               pltpu.VMEM((tb, PEER_HEADS * PEER_QDIM), F32),
               pltpu.VMEM((hk, tb), F32),
               pltpu.VMEM((hk, tb), I32),
               pltpu.VMEM((PEER_RING, tb, hk), F32),
               pltpu.VMEM((tb, hk), I32),
               pltpu.SMEM((PEER_RING, tb, hk), I32),
               pltpu.SemaphoreType.DMA((1,))]),
        compiler_params=pltpu.CompilerParams(
            dimension_semantics=("arbitrary",), vmem_limit_bytes=VMEM_LIMIT),
        name="peer",
    )(h1, h1, h1, wq_bf, keys_bf, uv3, g.reshape(1, d), b.reshape(1, d))


def kernel(x, w_in, b_in, conv_w, conv_b, conv_ln_g, conv_ln_b, w_out, b_out, ln1_g, ln1_b,
           peer_wq, peer_keys, peer_u, peer_v, ln2_g, ln2_b):
    batch, seq, d = x.shape
    depth = w_in.shape[0]
    alpha = (2.0 * depth) ** 0.25
    slopes = 2.0 ** (-8.0 * jnp.arange(1, N_HEADS + 1, dtype=F32) / N_HEADS)
    h = x.reshape(batch * seq, d)
    for l in range(depth):
        u, qkv = _in_proj(h, w_in[l].astype(BF16), b_in[l])
        yc = _conv(u, conv_w[l], conv_b[l], conv_ln_g[l], conv_ln_b[l], batch, seq)
        ya = _attention(qkv, slopes, batch, seq)
        h1 = _out_proj(yc, ya, h, w_out[l].astype(BF16), b_out[l], ln1_g[l], ln1_b[l], alpha)
        uv3 = jnp.concatenate([peer_u[l], peer_v[l]], axis=1).reshape(-1, 2 * d // LANES, LANES)
        h = _peer(h1, peer_wq[l].astype(BF16), peer_keys[l].astype(BF16), uv3,
                  ln2_g[l], ln2_b[l], alpha)
    return h.reshape(batch, seq, d)
```
